```python
import jax, jax.numpy as jnp
from jax import lax
import numpy as np

D_MODEL = 1024
BATCH = 4
SEQ = 4096
DEPTH = 2
DEC_BATCH = 32
DEC_SEQ = 1
PAST_LEN = 16384
PAGE_SIZE = 128

N_EVEN = (DEPTH + 1) // 2
N_ODD = DEPTH // 2
HEAD_DIM = 64
D_ATTN = D_MODEL // 2
N_HEADS_A = D_ATTN // HEAD_DIM
D_CONV = D_MODEL - D_ATTN
CONV_WIDTH = 31
D_IN_EVEN = 3 * D_ATTN + N_HEADS_A + 2 * D_CONV
D_SGU = D_MODEL
N_SGU_GROUPS = 8
SGU_GROUP = D_SGU // N_SGU_GROUPS
CHUNK = 128
D_FF = -(-8 * D_MODEL // (3 * 256)) * 256
BLOCK_Q = 128
RMS_EPS = 1e-6
LN_EPS = 1e-5
FORGET_BIAS_INIT = 4.0

kernel_name = "fox_conformer_gmlp_hybrid_step"


def rmsnorm(x, g):
    xf = x.astype(jnp.float32)
    y = xf * lax.rsqrt(jnp.mean(xf * xf, axis=-1, keepdims=True) + RMS_EPS)
    return (y * g.astype(jnp.float32)).astype(x.dtype)


def layernorm(x, g, b):
    xf = x.astype(jnp.float32)
    mu = jnp.mean(xf, axis=-1, keepdims=True)
    var = jnp.mean(jnp.square(xf - mu), axis=-1, keepdims=True)
    y = (xf - mu) * lax.rsqrt(var + LN_EPS) * g.astype(jnp.float32) + b.astype(jnp.float32)
    return y.astype(x.dtype)


def swiglu_ffn(h, w_gate, w_up, w_down):
    return (jax.nn.silu(h @ w_gate) * (h @ w_up)) @ w_down


def even_project(h, w_in, b_f):
    b, t, _ = h.shape
    p = h @ w_in
    q = p[..., :D_ATTN].reshape(b, t, N_HEADS_A, HEAD_DIM)
    k = p[..., D_ATTN:2 * D_ATTN].reshape(b, t, N_HEADS_A, HEAD_DIM)
    v = p[..., 2 * D_ATTN:3 * D_ATTN].reshape(b, t, N_HEADS_A, HEAD_DIM)
    o = 3 * D_ATTN
    logf = jax.nn.log_sigmoid(p[..., o:o + N_HEADS_A].astype(jnp.float32) + b_f.astype(jnp.float32))
    o = o + N_HEADS_A
    glu = p[..., o:o + D_CONV] * jax.nn.sigmoid(p[..., o + D_CONV:])
    return q, k, v, logf, glu


def fox_logits(q, k, c_q, c_k):
    s = jnp.einsum('bqhd,bkhd->bhqk', q, k, preferred_element_type=jnp.float32) * (HEAD_DIM ** -0.5)
    decay = jnp.swapaxes(c_q, 1, 2)[..., :, None] - jnp.swapaxes(c_k, 1, 2)[..., None, :]
    return s + decay


def fox_prompt(q, k, v, logf):
    c = jnp.cumsum(logf, axis=1)
    pos = jnp.arange(q.shape[1])
    outs = []
    for s0 in range(0, q.shape[1], BLOCK_Q):
        e = s0 + BLOCK_Q
        s = fox_logits(q[:, s0:e], k[:, :e], c[:, s0:e], c[:, :e])
        s = jnp.where(pos[:e][None, :] <= pos[s0:e][:, None], s, -jnp.inf)
        p = jax.nn.softmax(s, axis=-1).astype(v.dtype)
        outs.append(jnp.einsum('bhqk,bkhd->bqhd', p, v[:, :e]))
    return jnp.concatenate(outs, axis=1)


def fox_sample(q, k_new, v_new, logf_new, k_past, v_past, logf_past):
    c_past = jnp.cumsum(logf_past, axis=1)
    c_new = c_past[:, -1:] + jnp.cumsum(logf_new, axis=1)
    s_past = fox_logits(q, k_past, c_new, c_past)
    t = q.shape[1]
    causal = jnp.tril(jnp.ones((t, t), dtype=bool))
    s_new = jnp.where(causal, fox_logits(q, k_new, c_new, c_new), -jnp.inf)
    p = jax.nn.softmax(jnp.concatenate([s_past, s_new], axis=-1), axis=-1).astype(v_new.dtype)
    n_past = k_past.shape[1]
    return (jnp.einsum('bhqk,bkhd->bqhd', p[..., :n_past], v_past)
            + jnp.einsum('bhqk,bkhd->bqhd', p[..., n_past:], v_new))


def conformer_conv(glu_ext, conv_w, conv_b, ln_g, ln_b):
    y = lax.conv_general_dilated(glu_ext, conv_w[:, None, :].astype(glu_ext.dtype), window_strides=(1,),
                                 padding='VALID', dimension_numbers=('NWC', 'WIO', 'NWC'),
                                 feature_group_count=D_CONV)
    return jax.nn.silu(layernorm(y + conv_b, ln_g, ln_b))


def chunk_sgu_mixer(h, w_in, ln_g, ln_b, w_s, b_s, w_out):
    z = jax.nn.gelu(h @ w_in, approximate=False)
    u, v = z[..., :D_SGU], z[..., D_SGU:]
    vn = layernorm(v, ln_g, ln_b)
    b, t, _ = vn.shape
    pad = (-t) % CHUNK
    vp = jnp.pad(vn, ((0, 0), (0, pad), (0, 0))).reshape(b, (t + pad) // CHUNK, CHUNK, N_SGU_GROUPS, SGU_GROUP)
    mask = jnp.tril(jnp.ones((CHUNK, CHUNK), dtype=bool))
    w_c = jnp.where(mask, w_s, 0).astype(vp.dtype)
    mixed = jnp.einsum('gts,bnsgc->bntgc', w_c, vp) + jnp.swapaxes(b_s, 0, 1)[:, :, None]
    mixed = mixed.reshape(b, t + pad, D_SGU)[:, :t]
    return (u * mixed) @ w_out, vn


def setup_inputs(seed: int = 0) -> dict:
    key = jax.random.key(seed)
    ks = jax.random.split(key, 32)
    nrm = jax.random.normal
    n_pages = PAST_LEN // PAGE_SIZE
    n_phys = (5 * DEC_BATCH * n_pages + 3) // 4
    page_table = jax.random.permutation(ks[0], n_phys)[:DEC_BATCH * n_pages].reshape(DEC_BATCH, n_pages).astype(jnp.int32)
    row_scale = lax.rsqrt(jnp.arange(1, CHUNK + 1, dtype=jnp.float32))[:, None]
    return {
        'x_prompt': nrm(ks[1], (BATCH, SEQ, D_MODEL), jnp.float32),
        'x_sample': nrm(ks[2], (DEC_BATCH, DEC_SEQ, D_MODEL), jnp.float32),
        'cache_k': nrm(ks[3], (N_EVEN, n_phys, PAGE_SIZE, N_HEADS_A, HEAD_DIM), jnp.float32),
        'cache_v': nrm(ks[4], (N_EVEN, n_phys, PAGE_SIZE, N_HEADS_A, HEAD_DIM), jnp.float32),
        'cache_logf': jax.nn.log_sigmoid(FORGET_BIAS_INIT + nrm(ks[5], (N_EVEN, n_phys, PAGE_SIZE, N_HEADS_A), jnp.float32)),
        'page_table': page_table,
        'state_conv': 0.5 * nrm(ks[6], (N_EVEN, DEC_BATCH, CONV_WIDTH - 1, D_CONV), jnp.float32),
        'norm_mix': 1.0 + 0.02 * nrm(ks[7], (DEPTH, D_MODEL), jnp.float32),
        'norm_ffn': 1.0 + 0.02 * nrm(ks[8], (DEPTH, D_MODEL), jnp.float32),
        'norm_final': 1.0 + 0.02 * nrm(ks[9], (D_MODEL,), jnp.float32),
        'w_in_even': nrm(ks[10], (N_EVEN, D_MODEL, D_IN_EVEN), jnp.float32) * D_MODEL ** -0.5,
        'b_forget': FORGET_BIAS_INIT + 0.1 * nrm(ks[11], (N_EVEN, N_HEADS_A), jnp.float32),
        'conv_w': nrm(ks[12], (N_EVEN, CONV_WIDTH, D_CONV), jnp.float32) * CONV_WIDTH ** -0.5,
        'conv_b': 0.02 * nrm(ks[13], (N_EVEN, D_CONV), jnp.float32),
        'conv_ln_g': 1.0 + 0.02 * nrm(ks[14], (N_EVEN, D_CONV), jnp.float32),
        'conv_ln_b': 0.02 * nrm(ks[15], (N_EVEN, D_CONV), jnp.float32),
        'w_out_even': nrm(ks[16], (N_EVEN, D_ATTN + D_CONV, D_MODEL), jnp.float32) * (D_ATTN + D_CONV) ** -0.5,
        'w_in_odd': nrm(ks[17], (N_ODD, D_MODEL, 2 * D_SGU), jnp.float32) * D_MODEL ** -0.5,
        'sgu_ln_g': 1.0 + 0.02 * nrm(ks[18], (N_ODD, D_SGU), jnp.float32),
        'sgu_ln_b': 0.02 * nrm(ks[19], (N_ODD, D_SGU), jnp.float32),
        'sgu_w': nrm(ks[20], (N_ODD, N_SGU_GROUPS, CHUNK, CHUNK), jnp.float32) * row_scale,
        'sgu_b': 1.0 + 0.02 * nrm(ks[21], (N_ODD, N_SGU_GROUPS, CHUNK), jnp.float32),
        'w_out_odd': nrm(ks[22], (N_ODD, D_SGU, D_MODEL), jnp.float32) * D_SGU ** -0.5,
        'w_gate': nrm(ks[23], (DEPTH, D_MODEL, D_FF), jnp.float32) * D_MODEL ** -0.5,
        'w_up': nrm(ks[24], (DEPTH, D_MODEL, D_FF), jnp.float32) * D_MODEL ** -0.5,
        'w_down': nrm(ks[25], (DEPTH, D_FF, D_MODEL), jnp.float32) * D_FF ** -0.5,
    }


def reference(x_prompt, x_sample, cache_k, cache_v, cache_logf, page_table, state_conv,
              norm_mix, norm_ffn, norm_final, w_in_even, b_forget, conv_w, conv_b, conv_ln_g, conv_ln_b,
              w_out_even, w_in_odd, sgu_ln_g, sgu_ln_b, sgu_w, sgu_b, w_out_odd, w_gate, w_up, w_down):
    xp, xs = x_prompt, x_sample
    db = x_sample.shape[0]
    n_past = page_table.shape[1] * PAGE_SIZE
    kp_l, vp_l, fp_l, cp_l = [], [], [], []
    ks_l, vs_l, fs_l, cs_l = [], [], [], []
    sgu_l = []
    for l in range(DEPTH):
        hp = rmsnorm(xp, norm_mix[l])
        hs = rmsnorm(xs, norm_mix[l])
        if l % 2 == 0:
            i = l // 2
            q, k, v, logf, glu = even_project(hp, w_in_even[i], b_forget[i])
            attn = fox_prompt(q, k, v, logf).reshape(xp.shape[0], xp.shape[1], D_ATTN)
            conv = conformer_conv(jnp.pad(glu, ((0, 0), (CONV_WIDTH - 1, 0), (0, 0))),
                                  conv_w[i], conv_b[i], conv_ln_g[i], conv_ln_b[i])
            mp = jnp.concatenate([attn, conv], axis=-1) @ w_out_even[i]
            kp_l.append(k); vp_l.append(v); fp_l.append(logf); cp_l.append(glu[:, -(CONV_WIDTH - 1):])
            q2, k2, v2, logf2, glu2 = even_project(hs, w_in_even[i], b_forget[i])
            k_past = cache_k[i, page_table].reshape(db, n_past, N_HEADS_A, HEAD_DIM)
            v_past = cache_v[i, page_table].reshape(db, n_past, N_HEADS_A, HEAD_DIM)
            f_past = cache_logf[i, page_table].reshape(db, n_past, N_HEADS_A).astype(jnp.float32)
            attn2 = fox_sample(q2, k2, v2, logf2, k_past, v_past, f_past).reshape(db, xs.shape[1], D_ATTN)
            glu_ext = jnp.concatenate([state_conv[i].astype(glu2.dtype), glu2], axis=1)
            conv2 = conformer_conv(glu_ext, conv_w[i], conv_b[i], conv_ln_g[i], conv_ln_b[i])
            ms = jnp.concatenate([attn2, conv2], axis=-1) @ w_out_even[i]
            ks_l.append(k2); vs_l.append(v2); fs_l.append(logf2); cs_l.append(glu_ext[:, -(CONV_WIDTH - 1):])
        else:
            j = l // 2
            mp, _ = chunk_sgu_mixer(hp, w_in_odd[j], sgu_ln_g[j], sgu_ln_b[j], sgu_w[j], sgu_b[j], w_out_odd[j])
            ms, vn_s = chunk_sgu_mixer(hs, w_in_odd[j], sgu_ln_g[j], sgu_ln_b[j], sgu_w[j], sgu_b[j], w_out_odd[j])
            sgu_l.append(vn_s)
        xp = xp + mp
        xs = xs + ms
        xp = xp + swiglu_ffn(rmsnorm(xp, norm_ffn[l]), w_gate[l], w_up[l], w_down[l])
        xs = xs + swiglu_ffn(rmsnorm(xs, norm_ffn[l]), w_gate[l], w_up[l], w_down[l])
    y_prompt = rmsnorm(xp, norm_final)
    y_sample = rmsnorm(xs, norm_final)
    return (y_prompt, y_sample,
            jnp.stack(kp_l), jnp.stack(vp_l), jnp.stack(fp_l), jnp.stack(cp_l),
            jnp.stack(ks_l), jnp.stack(vs_l), jnp.stack(fs_l), jnp.stack(cs_l),
            jnp.stack(sgu_l))
```

```python
import functools

import jax
import jax.numpy as jnp
from jax import lax
from jax.experimental import pallas as pl
from jax.experimental.pallas import tpu as pltpu

HEAD_DIM = 64
CONV_WIDTH = 31
CHUNK = 128
N_SGU_GROUPS = 8
PAGE_SIZE = 128
RMS_EPS = 1e-6
LN_EPS = 1e-5

LANES = 128
SUBLANES = 8
VMEM_LIMIT = 56 * 1024 * 1024

ROW_TILE = 512
ATTN_TQ = 512
ATTN_TK = 512
CONV_HALO = 32
CONV_ROWS = 64
PAGES_PER_STEP = 8
NEG = -1e30

bf16 = jnp.bfloat16
f32 = jnp.float32


def _cparams(sem):
    return pltpu.CompilerParams(dimension_semantics=sem, vmem_limit_bytes=VMEM_LIMIT)


def _dot(a, b):
    return jnp.dot(a, b, preferred_element_type=f32)


def _dot_nt(a, b):
    return lax.dot_general(a, b, (((1,), (1,)), ((), ())), preferred_element_type=f32)


def _rms(x, g):
    ms = jnp.mean(x * x, axis=-1, keepdims=True)
    return x * lax.rsqrt(ms + RMS_EPS) * g


def _layernorm(x, g, b):
    mu = jnp.mean(x, axis=-1, keepdims=True)
    xc = x - mu
    var = jnp.mean(xc * xc, axis=-1, keepdims=True)
    return xc * lax.rsqrt(var + LN_EPS) * g + b


def _silu(x):
    return x * jax.nn.sigmoid(x)


def _gelu(x):
    return 0.5 * x * (1.0 + lax.erf(x * (2.0 ** -0.5)))


def _log_sigmoid(z):
    return jnp.minimum(z, 0.0) - jnp.log1p(jnp.exp(-jnp.abs(z)))


def _const_spec(shape):
    nd = len(shape)
    return pl.BlockSpec(shape, lambda *_: (0,) * nd)


def _row_spec(tm, width):
    return pl.BlockSpec((tm, width), lambda i: (i, 0))


def _even_in_kernel(x_ref, g_ref, wqkv_ref, wf_ref, wglu_ref, bf_ref,
                    q_ref, kb_ref, vb_ref, k_ref, v_ref, lf_ref, glu_ref, *, d_attn, d_conv, n_heads):
    h = _rms(x_ref[...], g_ref[...]).astype(bf16)
    q = _dot(h, wqkv_ref[:, 0:d_attn])
    q_ref[...] = (q * (HEAD_DIM ** -0.5)).astype(bf16)
    k = _dot(h, wqkv_ref[:, d_attn:2 * d_attn])
    k_ref[...] = k
    kb_ref[...] = k.astype(bf16)
    v = _dot(h, wqkv_ref[:, 2 * d_attn:3 * d_attn])
    v_ref[...] = v
    vb_ref[...] = v.astype(bf16)
    z = _dot(h, wf_ref[...]) + bf_ref[...]
    lf_ref[...] = _log_sigmoid(z)[:, 0:n_heads]
    a = _dot(h, wglu_ref[:, 0:d_conv])
    b = _dot(h, wglu_ref[:, d_conv:2 * d_conv])
    glu_ref[...] = a * jax.nn.sigmoid(b)


def _even_in(x, g, wqkv, wf, wglu, bfp, *, tm, n_heads):
    m, d = x.shape
    d_attn = wqkv.shape[1] // 3
    d_conv = wglu.shape[1] // 2
    kern = functools.partial(_even_in_kernel, d_attn=d_attn, d_conv=d_conv, n_heads=n_heads)
    out_shape = (
        jax.ShapeDtypeStruct((m, d_attn), bf16),
        jax.ShapeDtypeStruct((m, d_attn), bf16),
        jax.ShapeDtypeStruct((m, d_attn), bf16),
        jax.ShapeDtypeStruct((m, d_attn), f32),
        jax.ShapeDtypeStruct((m, d_attn), f32),
        jax.ShapeDtypeStruct((m, n_heads), f32),
        jax.ShapeDtypeStruct((m, d_conv), f32),
    )
    return pl.pallas_call(
        kern,
        grid=(m // tm,),
        in_specs=[_row_spec(tm, d), _const_spec(g.shape), _const_spec(wqkv.shape), _const_spec(wf.shape),
                  _const_spec(wglu.shape), _const_spec(bfp.shape)],
        out_specs=(_row_spec(tm, d_attn), _row_spec(tm, d_attn), _row_spec(tm, d_attn), _row_spec(tm, d_attn),
                   _row_spec(tm, d_attn), _row_spec(tm, n_heads), _row_spec(tm, d_conv)),
        out_shape=out_shape,
        compiler_params=_cparams(("parallel",)),
        name="even_in_proj",
    )(x, g, wqkv, wf, wglu, bfp)


def _cumsum_kernel(f_ref, c_ref):
    x = f_ref[...]
    n = x.shape[-1]
    lane = lax.broadcasted_iota(jnp.int32, x.shape, 1)
    d = 1
    while d < n:
        x = x + jnp.where(lane >= d, pltpu.roll(x, d, 1), 0.0)
        d *= 2
    c_ref[...] = x


def _cumsum_lanes(f_t):
    b, h, s = f_t.shape
    return pl.pallas_call(
        _cumsum_kernel,
        grid=(b,),
        in_specs=[pl.BlockSpec((None, h, s), lambda i: (i, 0, 0))],
        out_specs=pl.BlockSpec((None, h, s), lambda i: (i, 0, 0)),
        out_shape=jax.ShapeDtypeStruct((b, h, s), f32),
        compiler_params=_cparams(("parallel",)),
        name="logf_cumsum",
    )(f_t)


def _fox_prompt_kernel(q_ref, k_ref, v_ref, c_ref, o_ref, m_ref, l_ref, acc_ref, *, tq, tk):
    qi = pl.program_id(2)
    lane = lax.broadcasted_iota(jnp.int32, (tq, LANES), 1)
    first = lane < HEAD_DIM
    q = q_ref[...]
    zero = jnp.zeros_like(q)
    qh = (jnp.where(first, q, zero), jnp.where(first, zero, q))

    m_ref[...] = jnp.full(m_ref.shape, NEG, f32)
    l_ref[...] = jnp.zeros(l_ref.shape, f32)
    acc_ref[...] = jnp.zeros(acc_ref.shape, f32)

    def block(j, masked):
        start = pl.multiple_of(j * tk, tk)
        kb = k_ref[pl.ds(start, tk), :]
        vb = v_ref[pl.ds(start, tk), :]
        pv = []
        alphas = []
        for h in range(2):
            s = _dot_nt(qh[h], kb) - c_ref[h:h + 1, pl.ds(start, tk)]
            if masked:
                row = lax.broadcasted_iota(jnp.int32, (tq, tk), 0)
                col = lax.broadcasted_iota(jnp.int32, (tq, tk), 1)
                s = jnp.where(col <= row, s, NEG)
            m_prev = m_ref[h]
            m_new = jnp.maximum(m_prev, jnp.max(s, axis=1, keepdims=True))
            alpha = jnp.exp(m_prev - m_new)
            p = jnp.exp(s - pltpu.repeat(m_new, tk // LANES, axis=1))
            l_ref[h] = alpha * l_ref[h] + jnp.sum(p, axis=1, keepdims=True)
            m_ref[h] = m_new
            pv.append(_dot(p.astype(bf16), vb))
            alphas.append(alpha)
        acc_ref[...] = acc_ref[...] * jnp.where(first, alphas[0], alphas[1]) + jnp.where(first, pv[0], pv[1])

    def body(j, carry):
        block(j, False)
        return carry

    n_full = qi * (tq // tk)
    lax.fori_loop(0, n_full, body, 0)
    block(n_full, True)
    o_ref[...] = (acc_ref[...] / jnp.where(first, l_ref[0], l_ref[1])).astype(o_ref.dtype)


def _fox_prompt(q, k, v, c, *, batch, seq):
    n, d_attn = q.shape
    tq, tk = ATTN_TQ, ATTN_TK
    assert tq == tk and seq % tq == 0
    nq = seq // tq
    n_pairs = d_attn // LANES
    kern = functools.partial(_fox_prompt_kernel, tq=tq, tk=tk)
    return pl.pallas_call(
        kern,
        grid=(batch, n_pairs, nq),
        in_specs=[
            pl.BlockSpec((tq, LANES), lambda b, p, i: (b * nq + i, p)),
            pl.BlockSpec((seq, LANES), lambda b, p, i: (b, p)),
            pl.BlockSpec((seq, LANES), lambda b, p, i: (b, p)),
            pl.BlockSpec((None, None, 2, seq), lambda b, p, i: (b, p, 0, 0)),
        ],
        out_specs=pl.BlockSpec((tq, LANES), lambda b, p, i: (b * nq + i, p)),
        out_shape=jax.ShapeDtypeStruct((n, d_attn), bf16),
        scratch_shapes=[pltpu.VMEM((2, tq, LANES), f32), pltpu.VMEM((2, tq, LANES), f32),
                        pltpu.VMEM((tq, LANES), f32)],
        compiler_params=_cparams(("parallel", "parallel", "arbitrary")),
        name="fox_prompt_attention",
    )(q, k, v, c)


def _conv_ln_silu(y, cb, lg, lb):
    return _silu(_layernorm(y + cb, lg, lb))


def _even_out_kernel(x_ref, attn_ref, glu_ref, halo_ref, cw_ref, cb_ref, lg_ref, lb_ref, wa_ref, wc_ref,
                     o_ref, ext_ref, conv_ref, *, tm, tiles_per_seq):
    i = pl.program_id(0)
    starts_sequence = (i % tiles_per_seq) == 0
    ext_ref[0:CONV_HALO, :] = jnp.where(starts_sequence, 0.0, halo_ref[...])
    ext_ref[CONV_HALO:CONV_HALO + tm, :] = glu_ref[...]
    d_conv = glu_ref.shape[1]
    off = CONV_HALO - (CONV_WIDTH - 1)
    for c in range(d_conv // LANES):
        cs = slice(c * LANES, (c + 1) * LANES)
        for r in range(tm // CONV_ROWS):
            acc = jnp.zeros((CONV_ROWS, LANES), f32)
            for t in range(CONV_WIDTH):
                lo = r * CONV_ROWS + off + t
                acc = acc + cw_ref[t:t + 1, cs] * ext_ref[lo:lo + CONV_ROWS, cs]
            conv_ref[r * CONV_ROWS:(r + 1) * CONV_ROWS, cs] = acc
    conv = _conv_ln_silu(conv_ref[...], cb_ref[...], lg_ref[...], lb_ref[...]).astype(bf16)
    o_ref[...] = x_ref[...] + _dot(attn_ref[...], wa_ref[...]) + _dot(conv, wc_ref[...])


def _even_out(x, attn, glu, cw, cb, lg, lb, wa, wc, *, tm, seq):
    m, d = x.shape
    d_attn = attn.shape[1]
    d_conv = glu.shape[1]
    assert seq % tm == 0 and tm % CONV_HALO == 0
    tiles_per_seq = seq // tm
    halo_per_tile = tm // CONV_HALO
    kern = functools.partial(_even_out_kernel, tm=tm, tiles_per_seq=tiles_per_seq)
    return pl.pallas_call(
        kern,
        grid=(m // tm,),
        in_specs=[
            _row_spec(tm, d), _row_spec(tm, d_attn), _row_spec(tm, d_conv),
            pl.BlockSpec((CONV_HALO, d_conv), lambda i: (jnp.maximum(i * halo_per_tile - 1, 0), 0)),
            _const_spec(cw.shape), _const_spec(cb.shape), _const_spec(lg.shape), _const_spec(lb.shape),
            _const_spec(wa.shape), _const_spec(wc.shape),
        ],
        out_specs=_row_spec(tm, d),
        out_shape=jax.ShapeDtypeStruct((m, d), f32),
        scratch_shapes=[pltpu.VMEM((CONV_HALO + tm, d_conv), f32), pltpu.VMEM((tm, d_conv), f32)],
        compiler_params=_cparams(("parallel",)),
        name="even_conv_out_proj",
    )(x, attn, glu, glu, cw, cb, lg, lb, wa, wc)


def _even_out_sample_kernel(x_ref, attn_ref, glu_ref, st_ref, cw_ref, cb_ref, lg_ref, lb_ref, wa_ref, wc_ref,
                            o_ref):
    y = cw_ref[CONV_WIDTH - 1:CONV_WIDTH, :] * glu_ref[...]
    for t in range(CONV_WIDTH - 1):
        y = y + cw_ref[t:t + 1, :] * st_ref[t]
    conv = _conv_ln_silu(y, cb_ref[...], lg_ref[...], lb_ref[...]).astype(bf16)
    o_ref[...] = (x_ref[...] + _dot(attn_ref[...].astype(bf16), wa_ref[...]) + _dot(conv, wc_ref[...]))


def _even_out_sample(x, attn, glu, state_t, cw, cb, lg, lb, wa, wc):
    m, d = x.shape
    args = (x, attn, glu, state_t, cw, cb, lg, lb, wa, wc)
    return pl.pallas_call(
        _even_out_sample_kernel,
        grid=(1,),
        in_specs=[_const_spec(a.shape) for a in args],
        out_specs=_const_spec((m, d)),
        out_shape=jax.ShapeDtypeStruct((m, d), f32),
        compiler_params=_cparams(("arbitrary",)),
        name="even_conv_out_proj_sample",
    )(*args)


def _suffix_sums(f):
    lane = lax.broadcasted_iota(jnp.int32, f.shape, 1)
    x = f
    d = 1
    while d < PAGE_SIZE:
        x = x + jnp.where(lane + d < PAGE_SIZE, pltpu.roll(x, PAGE_SIZE - d, 1), 0.0)
        d *= 2
    return x - f


def _fox_sample_kernel(pt_ref, q_ref, kn_ref, vn_ref, lfn_ref, *refs, n_heads, pages):
    k_refs = refs[0:pages]
    v_refs = refs[pages:2 * pages]
    f_refs = refs[2 * pages:3 * pages]
    o_ref, m_ref, l_ref, acc_ref, carry_ref = refs[3 * pages:]
    j = pl.program_id(1)
    d_attn = q_ref.shape[-1]
    head_of_lane = lax.broadcasted_iota(jnp.int32, (n_heads, d_attn), 1) // HEAD_DIM
    own = head_of_lane == lax.broadcasted_iota(jnp.int32, (n_heads, d_attn), 0)
    q = q_ref[...].astype(f32)
    qbd32 = jnp.where(own, jnp.broadcast_to(q, (n_heads, d_attn)), 0.0)
    qbd = qbd32.astype(bf16)

    @pl.when(j == 0)
    def _():
        s_new = jnp.sum(qbd32 * kn_ref[...], axis=1, keepdims=True) - lfn_ref[...]
        m_ref[...] = s_new
        l_ref[...] = jnp.ones(l_ref.shape, f32)
        acc_ref[...] = jnp.broadcast_to(vn_ref[...], acc_ref.shape)
        carry_ref[...] = jnp.zeros(carry_ref.shape, f32)

    carry = carry_ref[...]
    scores = []
    for p in range(pages):
        f = f_refs[p][...]
        s = _dot_nt(qbd, k_refs[p][...].astype(bf16)) + (carry + _suffix_sums(f))
        scores.append(s)
        carry = carry + jnp.sum(f, axis=1, keepdims=True)
    carry_ref[...] = carry

    m_prev = m_ref[...]
    m_new = m_prev
    for s in scores:
        m_new = jnp.maximum(m_new, jnp.max(s, axis=1, keepdims=True))
    alpha = jnp.exp(m_prev - m_new)
    l_new = alpha * l_ref[...]
    acc = alpha * acc_ref[...]
    for p in range(pages):
        e = jnp.exp(scores[p] - m_new)
        l_new = l_new + jnp.sum(e, axis=1, keepdims=True)
        acc = acc + _dot(e.astype(bf16), v_refs[p][...].astype(bf16))
    m_ref[...] = m_new
    l_ref[...] = l_new
    acc_ref[...] = acc

    @pl.when(j == pl.num_programs(1) - 1)
    def _():
        o_ref[...] = jnp.sum(jnp.where(own, acc / l_new, 0.0), axis=0, keepdims=True)


def _fox_sample(page_table, q, k_new, v_new, lf_new, cache_k, cache_v, cache_f_t, layer):
    db, n_pages = page_table.shape
    n_heads = lf_new.shape[1]
    d_attn = q.shape[1]
    pages = PAGES_PER_STEP
    assert n_pages % pages == 0
    n_groups = n_pages // pages

    def page_map(p):
        return lambda b, j, pt: (layer, pt[b, n_pages - 1 - (j * pages + p)], 0, 0)

    row = lambda width: pl.BlockSpec((None, 1, width), lambda b, j, pt: (b, 0, 0))
    in_specs = [row(d_attn), row(d_attn), row(d_attn),
                pl.BlockSpec((None, n_heads, 1), lambda b, j, pt: (b, 0, 0))]
    in_specs += [pl.BlockSpec((None, None, PAGE_SIZE, d_attn), page_map(p)) for p in range(pages)]
    in_specs += [pl.BlockSpec((None, None, PAGE_SIZE, d_attn), page_map(p)) for p in range(pages)]
    in_specs += [pl.BlockSpec((None, None, n_heads, PAGE_SIZE), page_map(p)) for p in range(pages)]
    kern = functools.partial(_fox_sample_kernel, n_heads=n_heads, pages=pages)
    out = pl.pallas_call(
        kern,
        grid_spec=pltpu.PrefetchScalarGridSpec(
            num_scalar_prefetch=1,
            grid=(db, n_groups),
            in_specs=in_specs,
            out_specs=pl.BlockSpec((None, 1, d_attn), lambda b, j, pt: (b, 0, 0)),
            scratch_shapes=[pltpu.VMEM((n_heads, 1), f32), pltpu.VMEM((n_heads, 1), f32),
                            pltpu.VMEM((n_heads, d_attn), f32), pltpu.VMEM((n_heads, 1), f32)],
        ),
        out_shape=jax.ShapeDtypeStruct((db, 1, d_attn), f32),
        compiler_params=_cparams(("parallel", "arbitrary")),
        name="fox_sample_attention",
    )(page_table, q.reshape(db, 1, d_attn), k_new.reshape(db, 1, d_attn), v_new.reshape(db, 1, d_attn),
      lf_new.reshape(db, n_heads, 1), *([cache_k] * pages), *([cache_v] * pages), *([cache_f_t] * pages))
    return out.reshape(db, d_attn)


def _sgu_kernel(x_ref, g_ref, win_ref, lg_ref, lb_ref, ws_ref, bs_ref, wout_ref, *out_refs, tm, sample):
    d_sgu = wout_ref.shape[0]
    x = x_ref[...]
    h = _rms(x, g_ref[...]).astype(bf16)
    u = _gelu(_dot(h, win_ref[:, 0:d_sgu]))
    v = _gelu(_dot(h, win_ref[:, d_sgu:2 * d_sgu]))
    vn = _layernorm(v, lg_ref[...], lb_ref[...])
    if sample:
        o_ref, vn_ref = out_refs
        vn_ref[...] = vn
        gated = u * (vn * ws_ref[...] + bs_ref[...])
    else:
        (o_ref,) = out_refs
        vnb = vn.astype(bf16)
        group = d_sgu // N_SGU_GROUPS
        rows = []
        for r in range(tm // CHUNK):
            rs = slice(r * CHUNK, (r + 1) * CHUNK)
            cols = [_dot(ws_ref[g], vnb[rs, g * group:(g + 1) * group]) for g in range(N_SGU_GROUPS)]
            rows.append(jnp.concatenate(cols, axis=1) + bs_ref[...])
        mixed = jnp.concatenate(rows, axis=0)
        gated = u * mixed
    o_ref[...] = x + _dot(gated.astype(bf16), wout_ref[...])


def _sgu(x, g, win, lg, lb, ws, bs, wout, *, tm, sample):
    m, d = x.shape
    d_sgu = wout.shape[0]
    kern = functools.partial(_sgu_kernel, tm=tm, sample=sample)
    out_shape = [jax.ShapeDtypeStruct((m, d), f32)]
    out_specs = [_row_spec(tm, d)]
    if sample:
        out_shape.append(jax.ShapeDtypeStruct((m, d_sgu), f32))
        out_specs.append(_row_spec(tm, d_sgu))
    outs = pl.pallas_call(
        kern,
        grid=(m // tm,),
        in_specs=[_row_spec(tm, d), _const_spec(g.shape), _const_spec(win.shape), _const_spec(lg.shape),
                  _const_spec(lb.shape), _const_spec(ws.shape), _const_spec(bs.shape), _const_spec(wout.shape)],
        out_specs=tuple(out_specs),
        out_shape=tuple(out_shape),
        compiler_params=_cparams(("parallel",)),
        name="sgu_mixer_sample" if sample else "sgu_mixer",
    )(x, g, win, lg, lb, ws, bs, wout)
    return outs if sample else outs[0]


def _ffn_kernel(x_ref, g_ref, wg_ref, wu_ref, wd_ref, gf_ref, o_ref, *, tf, final):
    x = x_ref[...]
    h = _rms(x, g_ref[...]).astype(bf16)
    acc = x
    d_ff = wg_ref.shape[1]
    for c in range(d_ff // tf):
        sl = slice(c * tf, (c + 1) * tf)
        a = _silu(_dot(h, wg_ref[:, sl])) * _dot(h, wu_ref[:, sl])
        acc = acc + _dot(a.astype(bf16), wd_ref[sl, :])
    if final:
        acc = _rms(acc, gf_ref[...])
    o_ref[...] = acc


def _ffn(x, g, wg, wu, wd, gf, *, tm, final):
    m, d = x.shape
    d_ff = wg.shape[1]
    tf = d_ff // 2 if (d_ff // 2) % LANES == 0 else d_ff
    kern = functools.partial(_ffn_kernel, tf=tf, final=final)
    return pl.pallas_call(
        kern,
        grid=(m // tm,),
        in_specs=[_row_spec(tm, d), _const_spec(g.shape), _const_spec(wg.shape), _const_spec(wu.shape),
                  _const_spec(wd.shape), _const_spec(gf.shape)],
        out_specs=_row_spec(tm, d),
        out_shape=jax.ShapeDtypeStruct((m, d), f32),
        compiler_params=_cparams(("parallel",)),
        name="swiglu_ffn_final" if final else "swiglu_ffn",
    )(x, g, wg, wu, wd, gf)


def kernel(x_prompt, x_sample, cache_k, cache_v, cache_logf, page_table, state_conv, norm_mix, norm_ffn, norm_final, w_in_even, b_forget, conv_w, conv_b, conv_ln_g, conv_ln_b, w_out_even, w_in_odd, sgu_ln_g, sgu_ln_b, sgu_w, sgu_b, w_out_odd, w_gate, w_up, w_down):
    batch, seq, d_model = x_prompt.shape
    db, dec_seq, _ = x_sample.shape
    assert dec_seq == 1, "the sample path handles one new token per sequence"
    depth = norm_mix.shape[0]
    n_heads = b_forget.shape[1]
    d_attn = n_heads * HEAD_DIM
    d_conv = conv_w.shape[2]
    n_phys = cache_k.shape[1]
    n_tok = batch * seq
    row = lambda a: a.reshape(1, -1)

    xp = x_prompt.reshape(n_tok, d_model)
    xs = x_sample.reshape(db, d_model)
    tm_p = ROW_TILE
    tm_s = db

    cache_k4 = cache_k.reshape(cache_k.shape[0], n_phys, PAGE_SIZE, d_attn)
    cache_v4 = cache_v.reshape(cache_v.shape[0], n_phys, PAGE_SIZE, d_attn)
    cache_f_t = jnp.swapaxes(cache_logf.astype(f32), 2, 3)

    kp_l, vp_l, fp_l, cp_l = [], [], [], []
    ks_l, vs_l, fs_l, cs_l = [], [], [], []
    sgu_l = []
    for l in range(depth):
        g_mix = row(norm_mix[l])
        last = l == depth - 1
        if l % 2 == 0:
            i = l // 2
            w_in = w_in_even[i]
            wqkv = w_in[:, 0:3 * d_attn].astype(bf16)
            o = 3 * d_attn
            wf = jnp.pad(w_in[:, o:o + n_heads], ((0, 0), (0, LANES - n_heads))).astype(bf16)
            bfp = jnp.pad(row(b_forget[i]), ((0, 0), (0, LANES - n_heads)))
            wglu = w_in[:, o + n_heads:].astype(bf16)
            wa = w_out_even[i][0:d_attn].astype(bf16)
            wc = w_out_even[i][d_attn:].astype(bf16)
            cw, cb, lg, lb = conv_w[i], row(conv_b[i]), row(conv_ln_g[i]), row(conv_ln_b[i])

            qb, kb, vb, k, v, lf, glu = _even_in(xp, g_mix, wqkv, wf, wglu, bfp, tm=tm_p, n_heads=n_heads)
            lf_t = jnp.swapaxes(lf.reshape(batch, seq, n_heads), 1, 2)
            c = _cumsum_lanes(lf_t).reshape(batch, n_heads // 2, 2, seq)
            attn = _fox_prompt(qb, kb, vb, c, batch=batch, seq=seq)
            xp = _even_out(xp, attn, glu, cw, cb, lg, lb, wa, wc, tm=tm_p, seq=seq)
            kp_l.append(k.reshape(batch, seq, n_heads, HEAD_DIM))
            vp_l.append(v.reshape(batch, seq, n_heads, HEAD_DIM))
            fp_l.append(lf.reshape(batch, seq, n_heads))
            cp_l.append(glu.reshape(batch, seq, d_conv)[:, seq - (CONV_WIDTH - 1):])

            qb2, _, _, k2, v2, lf2, glu2 = _even_in(xs, g_mix, wqkv, wf, wglu, bfp, tm=tm_s, n_heads=n_heads)
            attn2 = _fox_sample(page_table, qb2, k2, v2, lf2, cache_k4, cache_v4, cache_f_t, i)
            state_t = jnp.swapaxes(state_conv[i], 0, 1)
            xs = _even_out_sample(xs, attn2, glu2, state_t, cw, cb, lg, lb, wa, wc)
            ks_l.append(k2.reshape(db, 1, n_heads, HEAD_DIM))
            vs_l.append(v2.reshape(db, 1, n_heads, HEAD_DIM))
            fs_l.append(lf2.reshape(db, 1, n_heads))
            cs_l.append(jnp.concatenate([state_conv[i][:, 1:], glu2[:, None, :]], axis=1))
        else:
            j = l // 2
            d_sgu = w_out_odd.shape[1]
            group = d_sgu // N_SGU_GROUPS
            win = w_in_odd[j].astype(bf16)
            wout = w_out_odd[j].astype(bf16)
            lg, lb = row(sgu_ln_g[j]), row(sgu_ln_b[j])
            causal = jnp.tril(jnp.ones((CHUNK, CHUNK), dtype=bool))
            ws = jnp.where(causal, sgu_w[j], 0).astype(bf16)
            bs = jnp.repeat(jnp.swapaxes(sgu_b[j], 0, 1), group, axis=1)
            xp = _sgu(xp, g_mix, win, lg, lb, ws, bs, wout, tm=tm_p, sample=False)
            ws0 = row(jnp.repeat(sgu_w[j][:, 0, 0], group))
            bs0 = row(jnp.repeat(sgu_b[j][:, 0], group))
            xs, vn_s = _sgu(xs, g_mix, win, lg, lb, ws0, bs0, wout, tm=tm_s, sample=True)
            sgu_l.append(vn_s.reshape(db, 1, d_sgu))
        g_ffn = row(norm_ffn[l])
        gf = row(norm_final)
        wg, wu, wd = w_gate[l].astype(bf16), w_up[l].astype(bf16), w_down[l].astype(bf16)
        xp = _ffn(xp, g_ffn, wg, wu, wd, gf, tm=tm_p, final=last)
        xs = _ffn(xs, g_ffn, wg, wu, wd, gf, tm=tm_s, final=last)

    return (xp.reshape(batch, seq, d_model), xs.reshape(db, 1, d_model),
            jnp.stack(kp_l), jnp.stack(vp_l), jnp.stack(fp_l), jnp.stack(cp_l),
            jnp.stack(ks_l), jnp.stack(vs_l), jnp.stack(fs_l), jnp.stack(cs_l),
            jnp.stack(sgu_l))
```

```python
import functools

import jax
import jax.numpy as jnp
from jax import lax
from jax.experimental import pallas as pl
from jax.experimental.pallas import tpu as pltpu

HEAD_DIM = 64
CONV_WIDTH = 31
CHUNK = 128
N_SGU_GROUPS = 8
PAGE_SIZE = 128
RMS_EPS = 1e-6
LN_EPS = 1e-5

LANES = 128
SUBLANES = 8
VMEM_LIMIT = 56 * 1024 * 1024

ROW_TILE = 512
ATTN_TQ = 512
ATTN_TK = 512
CONV_HALO = 32
CONV_ROWS = 64
PAGES_PER_STEP = 8
NEG = -1e30

bf16 = jnp.bfloat16
f32 = jnp.float32


def _cparams(sem):
    return pltpu.CompilerParams(dimension_semantics=sem, vmem_limit_bytes=VMEM_LIMIT)


def _dot(a, b):
    return jnp.dot(a, b, preferred_element_type=f32)


def _dot_nt(a, b):
    return lax.dot_general(a, b, (((1,), (1,)), ((), ())), preferred_element_type=f32)


def _rms(x, g):
    ms = jnp.mean(x * x, axis=-1, keepdims=True)
    return x * lax.rsqrt(ms + RMS_EPS) * g


def _layernorm(x, g, b):
    mu = jnp.mean(x, axis=-1, keepdims=True)
    xc = x - mu
    var = jnp.mean(xc * xc, axis=-1, keepdims=True)
    return xc * lax.rsqrt(var + LN_EPS) * g + b


def _silu(x):
    return x * jax.nn.sigmoid(x)


def _gelu(x):
    return 0.5 * x * (1.0 + lax.erf(x * (2.0 ** -0.5)))


def _log_sigmoid(z):
    return jnp.minimum(z, 0.0) - jnp.log1p(jnp.exp(-jnp.abs(z)))


def _const_spec(shape):
    nd = len(shape)
    return pl.BlockSpec(shape, lambda *_: (0,) * nd)


def _row_spec(tm, width):
    return pl.BlockSpec((tm, width), lambda i: (i, 0))


def _even_in_prompt_kernel(x_ref, g_ref, wq_ref, wkt_ref, wvt_ref, wft_ref, wglu_ref, bf_ref,
                           q_ref, ktb_ref, vtb_ref, kt_ref, vt_ref, lft_ref, glu_ref, *, d_conv):
    h = _rms(x_ref[...], g_ref[...]).astype(bf16)
    q_ref[...] = (_dot(h, wq_ref[...]) * (HEAD_DIM ** -0.5)).astype(bf16)
    kt = _dot_nt(wkt_ref[...], h)
    kt_ref[...] = kt
    ktb_ref[...] = kt.astype(bf16)
    vt = _dot_nt(wvt_ref[...], h)
    vt_ref[...] = vt
    vtb_ref[...] = vt.astype(bf16)
    lft_ref[...] = _log_sigmoid(_dot_nt(wft_ref[...], h) + bf_ref[...])
    a = _dot(h, wglu_ref[:, 0:d_conv])
    b = _dot(h, wglu_ref[:, d_conv:2 * d_conv])
    glu_ref[...] = a * jax.nn.sigmoid(b)


def _even_in_prompt(x, g, wq, wkt, wvt, wft, wglu, bf_col, *, tm, batch, seq):
    m, d = x.shape
    d_attn = wq.shape[1]
    n_heads = wft.shape[0]
    d_conv = wglu.shape[1] // 2
    tiles_per_seq = seq // tm
    kern = functools.partial(_even_in_prompt_kernel, d_conv=d_conv)
    t_spec = lambda rows: pl.BlockSpec((None, rows, tm), lambda i: (i // tiles_per_seq, 0, i % tiles_per_seq))
    out_shape = (
        jax.ShapeDtypeStruct((m, d_attn), bf16),
        jax.ShapeDtypeStruct((batch, d_attn, seq), bf16),
        jax.ShapeDtypeStruct((batch, d_attn, seq), bf16),
        jax.ShapeDtypeStruct((batch, d_attn, seq), f32),
        jax.ShapeDtypeStruct((batch, d_attn, seq), f32),
        jax.ShapeDtypeStruct((batch, n_heads, seq), f32),
        jax.ShapeDtypeStruct((m, d_conv), f32),
    )
    args = (x, g, wq, wkt, wvt, wft, wglu, bf_col)
    return pl.pallas_call(
        kern,
        grid=(m // tm,),
        in_specs=[_row_spec(tm, d)] + [_const_spec(a.shape) for a in args[1:]],
        out_specs=(_row_spec(tm, d_attn), t_spec(d_attn), t_spec(d_attn), t_spec(d_attn), t_spec(d_attn),
                   t_spec(n_heads), _row_spec(tm, d_conv)),
        out_shape=out_shape,
        compiler_params=_cparams(("parallel",)),
        name="even_in_proj",
    )(*args)


def _even_in_sample_kernel(x_ref, g_ref, wq_ref, wkt_ref, wvt_ref, wft_ref, wglu_ref, bf_ref,
                           q_ref, k_ref, v_ref, lf_ref, glu_ref, *, d_conv):
    h = _rms(x_ref[...], g_ref[...]).astype(bf16)
    q_ref[...] = (_dot(h, wq_ref[...]) * (HEAD_DIM ** -0.5)).astype(bf16)
    k_ref[...] = _dot_nt(h, wkt_ref[...])
    v_ref[...] = _dot_nt(h, wvt_ref[...])
    lf_ref[...] = _log_sigmoid(_dot_nt(h, wft_ref[...]) + bf_ref[...])
    a = _dot(h, wglu_ref[:, 0:d_conv])
    b = _dot(h, wglu_ref[:, d_conv:2 * d_conv])
    glu_ref[...] = a * jax.nn.sigmoid(b)


def _even_in_sample(x, g, wq, wkt, wvt, wft, wglu, bf_row):
    m, d = x.shape
    d_attn = wq.shape[1]
    n_heads = wft.shape[0]
    d_conv = wglu.shape[1] // 2
    kern = functools.partial(_even_in_sample_kernel, d_conv=d_conv)
    out_shape = (
        jax.ShapeDtypeStruct((m, d_attn), bf16),
        jax.ShapeDtypeStruct((m, d_attn), f32),
        jax.ShapeDtypeStruct((m, d_attn), f32),
        jax.ShapeDtypeStruct((m, n_heads), f32),
        jax.ShapeDtypeStruct((m, d_conv), f32),
    )
    args = (x, g, wq, wkt, wvt, wft, wglu, bf_row)
    return pl.pallas_call(
        kern,
        grid=(1,),
        in_specs=[_const_spec(a.shape) for a in args],
        out_specs=tuple(_const_spec(o.shape) for o in out_shape),
        out_shape=out_shape,
        compiler_params=_cparams(("arbitrary",)),
        name="even_in_proj_sample",
    )(*args)


def _cumsum_kernel(f_ref, c_ref):
    x = f_ref[...]
    n = x.shape[-1]
    lane = lax.broadcasted_iota(jnp.int32, x.shape, 1)
    d = 1
    while d < n:
        x = x + jnp.where(lane >= d, pltpu.roll(x, d, 1), 0.0)
        d *= 2
    c_ref[...] = x


def _cumsum_lanes(f_t):
    b, h, s = f_t.shape
    return pl.pallas_call(
        _cumsum_kernel,
        grid=(b,),
        in_specs=[pl.BlockSpec((None, h, s), lambda i: (i, 0, 0))],
        out_specs=pl.BlockSpec((None, h, s), lambda i: (i, 0, 0)),
        out_shape=jax.ShapeDtypeStruct((b, h, s), f32),
        compiler_params=_cparams(("parallel",)),
        name="logf_cumsum",
    )(f_t)


def _fox_prompt_kernel(q_ref, k_ref, v_ref, c_ref, o_ref, m_ref, l_ref, acc_ref, *, tq, tk):
    qi = pl.program_id(2)
    lane = lax.broadcasted_iota(jnp.int32, (tq, LANES), 1)
    first = lane < HEAD_DIM
    q = q_ref[...]
    zero = jnp.zeros_like(q)
    qh = (jnp.where(first, q, zero), jnp.where(first, zero, q))

    m_ref[...] = jnp.full(m_ref.shape, NEG, f32)
    l_ref[...] = jnp.zeros(l_ref.shape, f32)
    acc_ref[...] = jnp.zeros(acc_ref.shape, f32)

    def block(j, masked):
        start = pl.multiple_of(j * tk, tk)
        kb = k_ref[:, pl.ds(start, tk)]
        vb = v_ref[:, pl.ds(start, tk)]
        pv = []
        alphas = []
        for h in range(2):
            s = _dot(qh[h], kb) - c_ref[h:h + 1, pl.ds(start, tk)]
            if masked:
                row = lax.broadcasted_iota(jnp.int32, (tq, tk), 0)
                col = lax.broadcasted_iota(jnp.int32, (tq, tk), 1)
                s = jnp.where(col <= row, s, NEG)
            m_prev = m_ref[h]
            m_new = jnp.maximum(m_prev, jnp.max(s, axis=1, keepdims=True))
            alpha = jnp.exp(m_prev - m_new)
            p = jnp.exp(s - pltpu.repeat(m_new, tk // LANES, axis=1))
            l_ref[h] = alpha * l_ref[h] + jnp.sum(p, axis=1, keepdims=True)
            m_ref[h] = m_new
            pv.append(_dot_nt(p.astype(bf16), vb))
            alphas.append(alpha)
        acc_ref[...] = acc_ref[...] * jnp.where(first, alphas[0], alphas[1]) + jnp.where(first, pv[0], pv[1])

    def body(j, carry):
        block(j, False)
        return carry

    n_full = qi * (tq // tk)
    lax.fori_loop(0, n_full, body, 0)
    block(n_full, True)
    o_ref[...] = (acc_ref[...] / jnp.where(first, l_ref[0], l_ref[1])).astype(o_ref.dtype)


def _fox_prompt(q, kt, vt, c, *, batch, seq):
    n, d_attn = q.shape
    tq, tk = ATTN_TQ, ATTN_TK
    assert tq == tk and seq % tq == 0
    assert 2 * HEAD_DIM == LANES, "one 128-lane (or 128-row) block holds exactly two heads"
    nq = seq // tq
    n_pairs = d_attn // LANES
    kern = functools.partial(_fox_prompt_kernel, tq=tq, tk=tk)
    return pl.pallas_call(
        kern,
        grid=(batch, n_pairs, nq),
        in_specs=[
            pl.BlockSpec((tq, LANES), lambda b, p, i: (b * nq + i, p)),
            pl.BlockSpec((None, LANES, seq), lambda b, p, i: (b, p, 0)),
            pl.BlockSpec((None, LANES, seq), lambda b, p, i: (b, p, 0)),
            pl.BlockSpec((None, None, 2, seq), lambda b, p, i: (b, p, 0, 0)),
        ],
        out_specs=pl.BlockSpec((tq, LANES), lambda b, p, i: (b * nq + i, p)),
        out_shape=jax.ShapeDtypeStruct((n, d_attn), bf16),
        scratch_shapes=[pltpu.VMEM((2, tq, LANES), f32), pltpu.VMEM((2, tq, LANES), f32),
                        pltpu.VMEM((tq, LANES), f32)],
        compiler_params=_cparams(("parallel", "parallel", "arbitrary")),
        name="fox_prompt_attention",
    )(q, kt, vt, c)


def _conv_ln_silu(y, cb, lg, lb):
    return _silu(_layernorm(y + cb, lg, lb))


def _even_out_kernel(x_ref, attn_ref, glu_ref, halo_ref, cw_ref, cb_ref, lg_ref, lb_ref, wa_ref, wc_ref,
                     o_ref, ext_ref, sh_ref, conv_ref, *, tm, tiles_per_seq):
    i = pl.program_id(0)
    starts_sequence = (i % tiles_per_seq) == 0
    ext_ref[0:CONV_HALO, :] = jnp.where(starts_sequence, 0.0, halo_ref[...])
    ext_ref[CONV_HALO:CONV_HALO + tm, :] = glu_ref[...]
    d_conv = glu_ref.shape[1]
    off = CONV_HALO - (CONV_WIDTH - 1)
    sh_rows = sh_ref.shape[1]
    for b in range(1, SUBLANES):
        sh_ref[b - 1] = ext_ref[b:b + sh_rows, :]
    for c in range(d_conv // LANES):
        cs = slice(c * LANES, (c + 1) * LANES)
        for r in range(tm // CONV_ROWS):
            acc = jnp.zeros((CONV_ROWS, LANES), f32)
            for t in range(CONV_WIDTH):
                a, b = divmod(off + t, SUBLANES)
                lo = r * CONV_ROWS + a * SUBLANES
                src = ext_ref[lo:lo + CONV_ROWS, cs] if b == 0 else sh_ref[b - 1, lo:lo + CONV_ROWS, cs]
                acc = acc + cw_ref[t:t + 1, cs] * src
            conv_ref[r * CONV_ROWS:(r + 1) * CONV_ROWS, cs] = acc
    conv = _conv_ln_silu(conv_ref[...], cb_ref[...], lg_ref[...], lb_ref[...]).astype(bf16)
    o_ref[...] = x_ref[...] + _dot(attn_ref[...], wa_ref[...]) + _dot(conv, wc_ref[...])


def _even_out(x, attn, glu, cw, cb, lg, lb, wa, wc, *, tm, seq):
    m, d = x.shape
    d_attn = attn.shape[1]
    d_conv = glu.shape[1]
    assert seq % tm == 0 and tm % CONV_HALO == 0
    tiles_per_seq = seq // tm
    halo_per_tile = tm // CONV_HALO
    kern = functools.partial(_even_out_kernel, tm=tm, tiles_per_seq=tiles_per_seq)
    return pl.pallas_call(
        kern,
        grid=(m // tm,),
        in_specs=[
            _row_spec(tm, d), _row_spec(tm, d_attn), _row_spec(tm, d_conv),
            pl.BlockSpec((CONV_HALO, d_conv), lambda i: (jnp.maximum(i * halo_per_tile - 1, 0), 0)),
            _const_spec(cw.shape), _const_spec(cb.shape), _const_spec(lg.shape), _const_spec(lb.shape),
            _const_spec(wa.shape), _const_spec(wc.shape),
        ],
        out_specs=_row_spec(tm, d),
        out_shape=jax.ShapeDtypeStruct((m, d), f32),
        scratch_shapes=[pltpu.VMEM((CONV_HALO + tm, d_conv), f32),
                        pltpu.VMEM((SUBLANES - 1, CONV_HALO + tm - SUBLANES, d_conv), f32),
                        pltpu.VMEM((tm, d_conv), f32)],
        compiler_params=_cparams(("parallel",)),
        name="even_conv_out_proj",
    )(x, attn, glu, glu, cw, cb, lg, lb, wa, wc)


def _even_out_sample_kernel(x_ref, attn_ref, glu_ref, st_ref, cw_ref, cb_ref, lg_ref, lb_ref, wa_ref, wc_ref,
                            o_ref):
    y = cw_ref[CONV_WIDTH - 1:CONV_WIDTH, :] * glu_ref[...]
    for t in range(CONV_WIDTH - 1):
        y = y + cw_ref[t:t + 1, :] * st_ref[t]
    conv = _conv_ln_silu(y, cb_ref[...], lg_ref[...], lb_ref[...]).astype(bf16)
    o_ref[...] = (x_ref[...] + _dot(attn_ref[...].astype(bf16), wa_ref[...]) + _dot(conv, wc_ref[...]))


def _even_out_sample(x, attn, glu, state_t, cw, cb, lg, lb, wa, wc):
    m, d = x.shape
    args = (x, attn, glu, state_t, cw, cb, lg, lb, wa, wc)
    return pl.pallas_call(
        _even_out_sample_kernel,
        grid=(1,),
        in_specs=[_const_spec(a.shape) for a in args],
        out_specs=_const_spec((m, d)),
        out_shape=jax.ShapeDtypeStruct((m, d), f32),
        compiler_params=_cparams(("arbitrary",)),
        name="even_conv_out_proj_sample",
    )(*args)


def _suffix_sums(f):
    lane = lax.broadcasted_iota(jnp.int32, f.shape, 1)
    x = f
    d = 1
    while d < PAGE_SIZE:
        x = x + jnp.where(lane + d < PAGE_SIZE, pltpu.roll(x, PAGE_SIZE - d, 1), 0.0)
        d *= 2
    return x - f


def _fox_sample_kernel(pt_ref, q_ref, kn_ref, vn_ref, lfn_ref, *refs, n_heads, pages):
    k_refs = refs[0:pages]
    v_refs = refs[pages:2 * pages]
    f_refs = refs[2 * pages:3 * pages]
    o_ref, m_ref, l_ref, acc_ref, carry_ref = refs[3 * pages:]
    j = pl.program_id(1)
    d_attn = q_ref.shape[-1]
    head_of_lane = lax.broadcasted_iota(jnp.int32, (n_heads, d_attn), 1) // HEAD_DIM
    own = head_of_lane == lax.broadcasted_iota(jnp.int32, (n_heads, d_attn), 0)
    q = q_ref[...].astype(f32)
    qbd32 = jnp.where(own, jnp.broadcast_to(q, (n_heads, d_attn)), 0.0)
    qbd = qbd32.astype(bf16)

    @pl.when(j == 0)
    def _():
        s_new = jnp.sum(qbd32 * kn_ref[...], axis=1, keepdims=True) - lfn_ref[...]
        m_ref[...] = s_new
        l_ref[...] = jnp.ones(l_ref.shape, f32)
        acc_ref[...] = jnp.broadcast_to(vn_ref[...], acc_ref.shape)
        carry_ref[...] = jnp.zeros(carry_ref.shape, f32)

    carry = carry_ref[...]
    scores = []
    for p in range(pages):
        f = f_refs[p][...]
        s = _dot(qbd, k_refs[p][...].astype(bf16)) + (carry + _suffix_sums(f))
        scores.append(s)
        carry = carry + jnp.sum(f, axis=1, keepdims=True)
    carry_ref[...] = carry

    m_prev = m_ref[...]
    m_new = m_prev
    for s in scores:
        m_new = jnp.maximum(m_new, jnp.max(s, axis=1, keepdims=True))
    alpha = jnp.exp(m_prev - m_new)
    l_new = alpha * l_ref[...]
    acc = alpha * acc_ref[...]
    for p in range(pages):
        e = jnp.exp(scores[p] - m_new)
        l_new = l_new + jnp.sum(e, axis=1, keepdims=True)
        acc = acc + _dot_nt(e.astype(bf16), v_refs[p][...].astype(bf16))
    m_ref[...] = m_new
    l_ref[...] = l_new
    acc_ref[...] = acc

    @pl.when(j == pl.num_programs(1) - 1)
    def _():
        o_ref[...] = jnp.sum(jnp.where(own, acc / l_new, 0.0), axis=0, keepdims=True)


def _fox_sample(page_table, q, k_new, v_new, lf_new, cache_k, cache_v, cache_f_t, layer):
    db, n_pages = page_table.shape
    n_heads = lf_new.shape[1]
    d_attn = q.shape[1]
    pages = PAGES_PER_STEP
    assert n_pages % pages == 0
    n_groups = n_pages // pages

    def page_map(p):
        return lambda b, j, pt: (layer, pt[b, n_pages - 1 - (j * pages + p)], 0, 0)

    row = lambda width: pl.BlockSpec((None, 1, width), lambda b, j, pt: (b, 0, 0))
    in_specs = [row(d_attn), row(d_attn), row(d_attn),
                pl.BlockSpec((None, n_heads, 1), lambda b, j, pt: (b, 0, 0))]
    in_specs += [pl.BlockSpec((None, None, d_attn, PAGE_SIZE), page_map(p)) for p in range(pages)]
    in_specs += [pl.BlockSpec((None, None, d_attn, PAGE_SIZE), page_map(p)) for p in range(pages)]
    in_specs += [pl.BlockSpec((None, None, n_heads, PAGE_SIZE), page_map(p)) for p in range(pages)]
    kern = functools.partial(_fox_sample_kernel, n_heads=n_heads, pages=pages)
    out = pl.pallas_call(
        kern,
        grid_spec=pltpu.PrefetchScalarGridSpec(
            num_scalar_prefetch=1,
            grid=(db, n_groups),
            in_specs=in_specs,
            out_specs=pl.BlockSpec((None, 1, d_attn), lambda b, j, pt: (b, 0, 0)),
            scratch_shapes=[pltpu.VMEM((n_heads, 1), f32), pltpu.VMEM((n_heads, 1), f32),
                            pltpu.VMEM((n_heads, d_attn), f32), pltpu.VMEM((n_heads, 1), f32)],
        ),
        out_shape=jax.ShapeDtypeStruct((db, 1, d_attn), f32),
        compiler_params=_cparams(("parallel", "arbitrary")),
        name="fox_sample_attention",
    )(page_table, q.reshape(db, 1, d_attn), k_new.reshape(db, 1, d_attn), v_new.reshape(db, 1, d_attn),
      lf_new.reshape(db, n_heads, 1), *([cache_k] * pages), *([cache_v] * pages), *([cache_f_t] * pages))
    return out.reshape(db, d_attn)


def _sgu_kernel(x_ref, g_ref, win_ref, lg_ref, lb_ref, ws_ref, bs_ref, wout_ref, *out_refs, tm, sample):
    d_sgu = wout_ref.shape[0]
    x = x_ref[...]
    h = _rms(x, g_ref[...]).astype(bf16)
    u = _gelu(_dot(h, win_ref[:, 0:d_sgu]))
    v = _gelu(_dot(h, win_ref[:, d_sgu:2 * d_sgu]))
    vn = _layernorm(v, lg_ref[...], lb_ref[...])
    if sample:
        o_ref, vn_ref = out_refs
        vn_ref[...] = vn
        gated = u * (vn * ws_ref[...] + bs_ref[...])
    else:
        (o_ref,) = out_refs
        vnb = vn.astype(bf16)
        group = d_sgu // N_SGU_GROUPS
        rows = []
        for r in range(tm // CHUNK):
            rs = slice(r * CHUNK, (r + 1) * CHUNK)
            cols = [_dot(ws_ref[g], vnb[rs, g * group:(g + 1) * group]) for g in range(N_SGU_GROUPS)]
            rows.append(jnp.concatenate(cols, axis=1) + bs_ref[...])
        mixed = jnp.concatenate(rows, axis=0)
        gated = u * mixed
    o_ref[...] = x + _dot(gated.astype(bf16), wout_ref[...])


def _sgu(x, g, win, lg, lb, ws, bs, wout, *, tm, sample):
    m, d = x.shape
    d_sgu = wout.shape[0]
    kern = functools.partial(_sgu_kernel, tm=tm, sample=sample)
    out_shape = [jax.ShapeDtypeStruct((m, d), f32)]
    out_specs = [_row_spec(tm, d)]
    if sample:
        out_shape.append(jax.ShapeDtypeStruct((m, d_sgu), f32))
        out_specs.append(_row_spec(tm, d_sgu))
    outs = pl.pallas_call(
        kern,
        grid=(m // tm,),
        in_specs=[_row_spec(tm, d), _const_spec(g.shape), _const_spec(win.shape), _const_spec(lg.shape),
                  _const_spec(lb.shape), _const_spec(ws.shape), _const_spec(bs.shape), _const_spec(wout.shape)],
        out_specs=tuple(out_specs),
        out_shape=tuple(out_shape),
        compiler_params=_cparams(("parallel",)),
        name="sgu_mixer_sample" if sample else "sgu_mixer",
    )(x, g, win, lg, lb, ws, bs, wout)
    return outs if sample else outs[0]


def _ffn_kernel(x_ref, g_ref, wg_ref, wu_ref, wd_ref, gf_ref, o_ref, *, tf, final):
    x = x_ref[...]
    h = _rms(x, g_ref[...]).astype(bf16)
    acc = x
    d_ff = wg_ref.shape[1]
    for c in range(d_ff // tf):
        sl = slice(c * tf, (c + 1) * tf)
        a = _silu(_dot(h, wg_ref[:, sl])) * _dot(h, wu_ref[:, sl])
        acc = acc + _dot(a.astype(bf16), wd_ref[sl, :])
    if final:
        acc = _rms(acc, gf_ref[...])
    o_ref[...] = acc


def _ffn(x, g, wg, wu, wd, gf, *, tm, final):
    m, d = x.shape
    d_ff = wg.shape[1]
    tf = d_ff // 2 if (d_ff // 2) % LANES == 0 else d_ff
    kern = functools.partial(_ffn_kernel, tf=tf, final=final)
    return pl.pallas_call(
        kern,
        grid=(m // tm,),
        in_specs=[_row_spec(tm, d), _const_spec(g.shape), _const_spec(wg.shape), _const_spec(wu.shape),
                  _const_spec(wd.shape), _const_spec(gf.shape)],
        out_specs=_row_spec(tm, d),
        out_shape=jax.ShapeDtypeStruct((m, d), f32),
        compiler_params=_cparams(("parallel",)),
        name="swiglu_ffn_final" if final else "swiglu_ffn",
    )(x, g, wg, wu, wd, gf)


def kernel(x_prompt, x_sample, cache_k, cache_v, cache_logf, page_table, state_conv, norm_mix, norm_ffn, norm_final, w_in_even, b_forget, conv_w, conv_b, conv_ln_g, conv_ln_b, w_out_even, w_in_odd, sgu_ln_g, sgu_ln_b, sgu_w, sgu_b, w_out_odd, w_gate, w_up, w_down):
    batch, seq, d_model = x_prompt.shape
    db, dec_seq, _ = x_sample.shape
    assert dec_seq == 1, "the sample path handles one new token per sequence"
    depth = norm_mix.shape[0]
    n_heads = b_forget.shape[1]
    d_attn = n_heads * HEAD_DIM
    d_conv = conv_w.shape[2]
    n_phys = cache_k.shape[1]
    n_tok = batch * seq
    row = lambda a: a.reshape(1, -1)

    xp = x_prompt.reshape(n_tok, d_model)
    xs = x_sample.reshape(db, d_model)
    tm_p = ROW_TILE
    tm_s = db

    feature_major = lambda a: jnp.transpose(a, (0, 1, 3, 4, 2)).reshape(a.shape[0], n_phys, d_attn, PAGE_SIZE)
    cache_kt = feature_major(cache_k)
    cache_vt = feature_major(cache_v)
    cache_f_t = jnp.swapaxes(cache_logf.astype(f32), 2, 3)

    kp_l, vp_l, fp_l, cp_l = [], [], [], []
    ks_l, vs_l, fs_l, cs_l = [], [], [], []
    sgu_l = []
    for l in range(depth):
        g_mix = row(norm_mix[l])
        last = l == depth - 1
        if l % 2 == 0:
            i = l // 2
            w_in = w_in_even[i]
            w_in_t = w_in.T
            wq = w_in[:, 0:d_attn].astype(bf16)
            wkt = w_in_t[d_attn:2 * d_attn].astype(bf16)
            wvt = w_in_t[2 * d_attn:3 * d_attn].astype(bf16)
            o = 3 * d_attn
            wft = w_in_t[o:o + n_heads].astype(bf16)
            wglu = w_in[:, o + n_heads:].astype(bf16)
            wa = w_out_even[i][0:d_attn].astype(bf16)
            wc = w_out_even[i][d_attn:].astype(bf16)
            cw, cb, lg, lb = conv_w[i], row(conv_b[i]), row(conv_ln_g[i]), row(conv_ln_b[i])

            qb, ktb, vtb, kt, vt, lf_t, glu = _even_in_prompt(
                xp, g_mix, wq, wkt, wvt, wft, wglu, b_forget[i].reshape(n_heads, 1), tm=tm_p, batch=batch, seq=seq)
            c = _cumsum_lanes(lf_t).reshape(batch, n_heads // 2, 2, seq)
            attn = _fox_prompt(qb, ktb, vtb, c, batch=batch, seq=seq)
            xp = _even_out(xp, attn, glu, cw, cb, lg, lb, wa, wc, tm=tm_p, seq=seq)
            token_major = lambda a: jnp.transpose(a.reshape(batch, n_heads, HEAD_DIM, seq), (0, 3, 1, 2))
            kp_l.append(token_major(kt))
            vp_l.append(token_major(vt))
            fp_l.append(jnp.swapaxes(lf_t, 1, 2))
            cp_l.append(glu.reshape(batch, seq, d_conv)[:, seq - (CONV_WIDTH - 1):])

            qb2, k2, v2, lf2, glu2 = _even_in_sample(xs, g_mix, wq, wkt, wvt, wft, wglu, row(b_forget[i]))
            attn2 = _fox_sample(page_table, qb2, k2, v2, lf2, cache_kt, cache_vt, cache_f_t, i)
            state_t = jnp.swapaxes(state_conv[i], 0, 1)
            xs = _even_out_sample(xs, attn2, glu2, state_t, cw, cb, lg, lb, wa, wc)
            ks_l.append(k2.reshape(db, 1, n_heads, HEAD_DIM))
            vs_l.append(v2.reshape(db, 1, n_heads, HEAD_DIM))
            fs_l.append(lf2.reshape(db, 1, n_heads))
            cs_l.append(jnp.concatenate([state_conv[i][:, 1:], glu2[:, None, :]], axis=1))
        else:
            j = l // 2
            d_sgu = w_out_odd.shape[1]
            group = d_sgu // N_SGU_GROUPS
            win = w_in_odd[j].astype(bf16)
            wout = w_out_odd[j].astype(bf16)
            lg, lb = row(sgu_ln_g[j]), row(sgu_ln_b[j])
            causal = jnp.tril(jnp.ones((CHUNK, CHUNK), dtype=bool))
            ws = jnp.where(causal, sgu_w[j], 0).astype(bf16)
            bs = jnp.repeat(jnp.swapaxes(sgu_b[j], 0, 1), group, axis=1)
            xp = _sgu(xp, g_mix, win, lg, lb, ws, bs, wout, tm=tm_p, sample=False)
            ws0 = row(jnp.repeat(sgu_w[j][:, 0, 0], group))
            bs0 = row(jnp.repeat(sgu_b[j][:, 0], group))
            xs, vn_s = _sgu(xs, g_mix, win, lg, lb, ws0, bs0, wout, tm=tm_s, sample=True)
            sgu_l.append(vn_s.reshape(db, 1, d_sgu))
        g_ffn = row(norm_ffn[l])
        gf = row(norm_final)
        wg, wu, wd = w_gate[l].astype(bf16), w_up[l].astype(bf16), w_down[l].astype(bf16)
        xp = _ffn(xp, g_ffn, wg, wu, wd, gf, tm=tm_p, final=last)
        xs = _ffn(xs, g_ffn, wg, wu, wd, gf, tm=tm_s, final=last)

    return (xp.reshape(batch, seq, d_model), xs.reshape(db, 1, d_model),
            jnp.stack(kp_l), jnp.stack(vp_l), jnp.stack(fp_l), jnp.stack(cp_l),
            jnp.stack(ks_l), jnp.stack(vs_l), jnp.stack(fs_l), jnp.stack(cs_l),
            jnp.stack(sgu_l))
```

```python
import functools

import jax
import jax.numpy as jnp
from jax import lax
from jax.experimental import pallas as pl
from jax.experimental.pallas import tpu as pltpu

HEAD_DIM = 64
CONV_WIDTH = 31
CHUNK = 128
N_SGU_GROUPS = 8
PAGE_SIZE = 128
RMS_EPS = 1e-6
LN_EPS = 1e-5

LANES = 128
SUBLANES = 8
VMEM_LIMIT = 56 * 1024 * 1024

ROW_TILE = 512
ATTN_TQ = 512
ATTN_TK = 512
CONV_HALO = 32
CONV_ROWS = 64
PAGES_PER_STEP = 16
NEG = -1e30

bf16 = jnp.bfloat16
f32 = jnp.float32


def _cparams(sem):
    return pltpu.CompilerParams(dimension_semantics=sem, vmem_limit_bytes=VMEM_LIMIT)


def _dot(a, b):
    return jnp.dot(a, b, preferred_element_type=f32)


def _dot_nt(a, b):
    return lax.dot_general(a, b, (((1,), (1,)), ((), ())), preferred_element_type=f32)


def _rms(x, g):
    ms = jnp.mean(x * x, axis=-1, keepdims=True)
    return x * lax.rsqrt(ms + RMS_EPS) * g


def _layernorm(x, g, b):
    mu = jnp.mean(x, axis=-1, keepdims=True)
    xc = x - mu
    var = jnp.mean(xc * xc, axis=-1, keepdims=True)
    return xc * lax.rsqrt(var + LN_EPS) * g + b


def _silu(x):
    return x * jax.nn.sigmoid(x)


def _gelu(x):
    return 0.5 * x * (1.0 + lax.erf(x * (2.0 ** -0.5)))


def _log_sigmoid(z):
    return jnp.minimum(z, 0.0) - jnp.log1p(jnp.exp(-jnp.abs(z)))


def _const_spec(shape):
    nd = len(shape)
    return pl.BlockSpec(shape, lambda *_: (0,) * nd)


def _row_spec(tm, width):
    return pl.BlockSpec((tm, width), lambda i: (i, 0))


def _even_in_prompt_kernel(x_ref, g_ref, wq_ref, wkt_ref, wvt_ref, wft_ref, wglu_ref, bf_ref,
                           q_ref, ktb_ref, vtb_ref, kt_ref, vt_ref, lft_ref, glu_ref, *, d_conv):
    h = _rms(x_ref[...], g_ref[...]).astype(bf16)
    q_ref[...] = (_dot(h, wq_ref[...]) * (HEAD_DIM ** -0.5)).astype(bf16)
    kt = _dot_nt(wkt_ref[...], h)
    kt_ref[...] = kt
    ktb_ref[...] = kt.astype(bf16)
    vt = _dot_nt(wvt_ref[...], h)
    vt_ref[...] = vt
    vtb_ref[...] = vt.astype(bf16)
    lft_ref[...] = _log_sigmoid(_dot_nt(wft_ref[...], h) + bf_ref[...])
    a = _dot(h, wglu_ref[:, 0:d_conv])
    b = _dot(h, wglu_ref[:, d_conv:2 * d_conv])
    glu_ref[...] = a * jax.nn.sigmoid(b)


def _even_in_prompt(x, g, wq, wkt, wvt, wft, wglu, bf_col, *, tm, batch, seq):
    m, d = x.shape
    d_attn = wq.shape[1]
    n_heads = wft.shape[0]
    d_conv = wglu.shape[1] // 2
    tiles_per_seq = seq // tm
    kern = functools.partial(_even_in_prompt_kernel, d_conv=d_conv)
    t_spec = lambda rows: pl.BlockSpec((None, rows, tm), lambda i: (i // tiles_per_seq, 0, i % tiles_per_seq))
    out_shape = (
        jax.ShapeDtypeStruct((m, d_attn), bf16),
        jax.ShapeDtypeStruct((batch, d_attn, seq), bf16),
        jax.ShapeDtypeStruct((batch, d_attn, seq), bf16),
        jax.ShapeDtypeStruct((batch, d_attn, seq), f32),
        jax.ShapeDtypeStruct((batch, d_attn, seq), f32),
        jax.ShapeDtypeStruct((batch, n_heads, seq), f32),
        jax.ShapeDtypeStruct((m, d_conv), f32),
    )
    args = (x, g, wq, wkt, wvt, wft, wglu, bf_col)
    return pl.pallas_call(
        kern,
        grid=(m // tm,),
        in_specs=[_row_spec(tm, d)] + [_const_spec(a.shape) for a in args[1:]],
        out_specs=(_row_spec(tm, d_attn), t_spec(d_attn), t_spec(d_attn), t_spec(d_attn), t_spec(d_attn),
                   t_spec(n_heads), _row_spec(tm, d_conv)),
        out_shape=out_shape,
        compiler_params=_cparams(("parallel",)),
        name="even_in_proj",
    )(*args)


def _even_in_sample_kernel(x_ref, g_ref, wq_ref, wkt_ref, wvt_ref, wft_ref, wglu_ref, bf_ref,
                           q_ref, k_ref, v_ref, lf_ref, glu_ref, *, d_conv):
    h = _rms(x_ref[...], g_ref[...]).astype(bf16)
    q_ref[...] = (_dot(h, wq_ref[...]) * (HEAD_DIM ** -0.5)).astype(bf16)
    k_ref[...] = _dot_nt(h, wkt_ref[...])
    v_ref[...] = _dot_nt(h, wvt_ref[...])
    lf_ref[...] = _log_sigmoid(_dot_nt(h, wft_ref[...]) + bf_ref[...])
    a = _dot(h, wglu_ref[:, 0:d_conv])
    b = _dot(h, wglu_ref[:, d_conv:2 * d_conv])
    glu_ref[...] = a * jax.nn.sigmoid(b)


def _even_in_sample(x, g, wq, wkt, wvt, wft, wglu, bf_row):
    m, d = x.shape
    d_attn = wq.shape[1]
    n_heads = wft.shape[0]
    d_conv = wglu.shape[1] // 2
    kern = functools.partial(_even_in_sample_kernel, d_conv=d_conv)
    out_shape = (
        jax.ShapeDtypeStruct((m, d_attn), bf16),
        jax.ShapeDtypeStruct((m, d_attn), f32),
        jax.ShapeDtypeStruct((m, d_attn), f32),
        jax.ShapeDtypeStruct((m, n_heads), f32),
        jax.ShapeDtypeStruct((m, d_conv), f32),
    )
    args = (x, g, wq, wkt, wvt, wft, wglu, bf_row)
    return pl.pallas_call(
        kern,
        grid=(1,),
        in_specs=[_const_spec(a.shape) for a in args],
        out_specs=tuple(_const_spec(o.shape) for o in out_shape),
        out_shape=out_shape,
        compiler_params=_cparams(("arbitrary",)),
        name="even_in_proj_sample",
    )(*args)


def _cumsum_kernel(f_ref, c_ref):
    x = f_ref[...]
    n = x.shape[-1]
    lane = lax.broadcasted_iota(jnp.int32, x.shape, 1)
    d = 1
    while d < n:
        x = x + jnp.where(lane >= d, pltpu.roll(x, d, 1), 0.0)
        d *= 2
    c_ref[...] = x


def _cumsum_lanes(f_t):
    b, h, s = f_t.shape
    return pl.pallas_call(
        _cumsum_kernel,
        grid=(b,),
        in_specs=[pl.BlockSpec((None, h, s), lambda i: (i, 0, 0))],
        out_specs=pl.BlockSpec((None, h, s), lambda i: (i, 0, 0)),
        out_shape=jax.ShapeDtypeStruct((b, h, s), f32),
        compiler_params=_cparams(("parallel",)),
        name="logf_cumsum",
    )(f_t)


def _fox_prompt_kernel(q_ref, k_ref, v_ref, c_ref, o_ref, m_ref, acc_ref, *, tq, tk, n_heads):
    qi = pl.program_id(1)
    pair = 2 * HEAD_DIM
    lane = lax.broadcasted_iota(jnp.int32, (tq, pair), 1)
    first = lane < HEAD_DIM
    vrow = lax.broadcasted_iota(jnp.int32, (pair, tk), 0)
    sum_lane = (HEAD_DIM, 0)
    own_rows = (vrow < HEAD_DIM, vrow >= HEAD_DIM)
    keep = [jnp.where(own_rows[i], 1.0, 0.0).astype(bf16) for i in range(2)]
    ones_row = [jnp.where(vrow == sum_lane[i], 1.0, 0.0).astype(bf16) for i in range(2)]

    qm = []
    for h in range(n_heads):
        q = q_ref[:, (h // 2) * pair:(h // 2 + 1) * pair]
        zero = jnp.zeros_like(q)
        qm.append(jnp.where(first, q, zero) if h % 2 == 0 else jnp.where(first, zero, q))

    m_ref[...] = jnp.full(m_ref.shape, NEG, f32)
    acc_ref[...] = jnp.zeros(acc_ref.shape, f32)

    def block(j, masked):
        start = pl.multiple_of(j * tk, tk)
        for h in range(n_heads):
            rows = slice((h // 2) * pair, (h // 2 + 1) * pair)
            kb = k_ref[rows, pl.ds(start, tk)]
            vb = v_ref[rows, pl.ds(start, tk)]
            va = vb * keep[h % 2] + ones_row[h % 2]
            s = _dot(qm[h], kb) - c_ref[h:h + 1, pl.ds(start, tk)]
            if masked:
                row = lax.broadcasted_iota(jnp.int32, (tq, tk), 0)
                col = lax.broadcasted_iota(jnp.int32, (tq, tk), 1)
                s = jnp.where(col <= row, s, NEG)
            m_prev = m_ref[h]
            m_new = jnp.maximum(m_prev, jnp.max(s, axis=1, keepdims=True))
            p = jnp.exp(s - jnp.concatenate([m_new] * (tk // pair), axis=1))
            m_ref[h] = m_new
            acc_ref[h] = jnp.exp(m_prev - m_new) * acc_ref[h] + _dot_nt(p.astype(bf16), va)

    def body(j, carry):
        block(j, False)
        return carry

    n_full = qi * (tq // tk)
    lax.fori_loop(0, n_full, body, 0)
    block(n_full, True)
    for h in range(0, n_heads, 2):
        a0, a1 = acc_ref[h], acc_ref[h + 1]
        o0 = a0 / a0[:, sum_lane[0]:sum_lane[0] + 1]
        o1 = a1 / a1[:, sum_lane[1]:sum_lane[1] + 1]
        o_ref[:, (h // 2) * pair:(h // 2 + 1) * pair] = jnp.where(first, o0, o1).astype(o_ref.dtype)


def _fox_prompt(q, kt, vt, c, *, batch, seq):
    n, d_attn = q.shape
    n_heads = c.shape[1]
    tq, tk = ATTN_TQ, ATTN_TK
    assert tq == tk and seq % tq == 0
    assert 2 * HEAD_DIM == LANES and n_heads % 2 == 0, "one 128-wide block holds exactly two heads"
    nq = seq // tq
    kern = functools.partial(_fox_prompt_kernel, tq=tq, tk=tk, n_heads=n_heads)
    return pl.pallas_call(
        kern,
        grid=(batch, nq),
        in_specs=[
            pl.BlockSpec((tq, d_attn), lambda b, i: (b * nq + i, 0)),
            pl.BlockSpec((None, d_attn, seq), lambda b, i: (b, 0, 0)),
            pl.BlockSpec((None, d_attn, seq), lambda b, i: (b, 0, 0)),
            pl.BlockSpec((None, n_heads, seq), lambda b, i: (b, 0, 0)),
        ],
        out_specs=pl.BlockSpec((tq, d_attn), lambda b, i: (b * nq + i, 0)),
        out_shape=jax.ShapeDtypeStruct((n, d_attn), bf16),
        scratch_shapes=[pltpu.VMEM((n_heads, tq, LANES), f32), pltpu.VMEM((n_heads, tq, LANES), f32)],
        compiler_params=_cparams(("parallel", "arbitrary")),
        name="fox_prompt_attention",
    )(q, kt, vt, c)


def _conv_ln_silu(y, cb, lg, lb):
    return _silu(_layernorm(y + cb, lg, lb))


def _even_out_kernel(x_ref, attn_ref, glu_ref, halo_ref, cw_ref, cb_ref, lg_ref, lb_ref, wa_ref, wc_ref,
                     o_ref, ext_ref, sh_ref, conv_ref, *, tm, tiles_per_seq):
    i = pl.program_id(0)
    starts_sequence = (i % tiles_per_seq) == 0
    ext_ref[0:CONV_HALO, :] = jnp.where(starts_sequence, 0.0, halo_ref[...])
    ext_ref[CONV_HALO:CONV_HALO + tm, :] = glu_ref[...]
    d_conv = glu_ref.shape[1]
    off = CONV_HALO - (CONV_WIDTH - 1)
    sh_rows = sh_ref.shape[1]
    for b in range(1, SUBLANES):
        sh_ref[b - 1] = ext_ref[b:b + sh_rows, :]
    for c in range(d_conv // LANES):
        cs = slice(c * LANES, (c + 1) * LANES)
        for r in range(tm // CONV_ROWS):
            acc = jnp.zeros((CONV_ROWS, LANES), f32)
            for t in range(CONV_WIDTH):
                a, b = divmod(off + t, SUBLANES)
                lo = r * CONV_ROWS + a * SUBLANES
                src = ext_ref[lo:lo + CONV_ROWS, cs] if b == 0 else sh_ref[b - 1, lo:lo + CONV_ROWS, cs]
                acc = acc + cw_ref[t:t + 1, cs] * src
            conv_ref[r * CONV_ROWS:(r + 1) * CONV_ROWS, cs] = acc
    conv = _conv_ln_silu(conv_ref[...], cb_ref[...], lg_ref[...], lb_ref[...]).astype(bf16)
    o_ref[...] = x_ref[...] + _dot(attn_ref[...], wa_ref[...]) + _dot(conv, wc_ref[...])


def _even_out(x, attn, glu, cw, cb, lg, lb, wa, wc, *, tm, seq):
    m, d = x.shape
    d_attn = attn.shape[1]
    d_conv = glu.shape[1]
    assert seq % tm == 0 and tm % CONV_HALO == 0
    tiles_per_seq = seq // tm
    halo_per_tile = tm // CONV_HALO
    kern = functools.partial(_even_out_kernel, tm=tm, tiles_per_seq=tiles_per_seq)
    return pl.pallas_call(
        kern,
        grid=(m // tm,),
        in_specs=[
            _row_spec(tm, d), _row_spec(tm, d_attn), _row_spec(tm, d_conv),
            pl.BlockSpec((CONV_HALO, d_conv), lambda i: (jnp.maximum(i * halo_per_tile - 1, 0), 0)),
            _const_spec(cw.shape), _const_spec(cb.shape), _const_spec(lg.shape), _const_spec(lb.shape),
            _const_spec(wa.shape), _const_spec(wc.shape),
        ],
        out_specs=_row_spec(tm, d),
        out_shape=jax.ShapeDtypeStruct((m, d), f32),
        scratch_shapes=[pltpu.VMEM((CONV_HALO + tm, d_conv), f32),
                        pltpu.VMEM((SUBLANES - 1, CONV_HALO + tm - SUBLANES, d_conv), f32),
                        pltpu.VMEM((tm, d_conv), f32)],
        compiler_params=_cparams(("parallel",)),
        name="even_conv_out_proj",
    )(x, attn, glu, glu, cw, cb, lg, lb, wa, wc)


def _even_out_sample_kernel(x_ref, attn_ref, glu_ref, st_ref, cw_ref, cb_ref, lg_ref, lb_ref, wa_ref, wc_ref,
                            o_ref):
    y = cw_ref[CONV_WIDTH - 1:CONV_WIDTH, :] * glu_ref[...]
    for t in range(CONV_WIDTH - 1):
        y = y + cw_ref[t:t + 1, :] * st_ref[t]
    conv = _conv_ln_silu(y, cb_ref[...], lg_ref[...], lb_ref[...]).astype(bf16)
    o_ref[...] = (x_ref[...] + _dot(attn_ref[...].astype(bf16), wa_ref[...]) + _dot(conv, wc_ref[...]))


def _even_out_sample(x, attn, glu, state_t, cw, cb, lg, lb, wa, wc):
    m, d = x.shape
    args = (x, attn, glu, state_t, cw, cb, lg, lb, wa, wc)
    return pl.pallas_call(
        _even_out_sample_kernel,
        grid=(1,),
        in_specs=[_const_spec(a.shape) for a in args],
        out_specs=_const_spec((m, d)),
        out_shape=jax.ShapeDtypeStruct((m, d), f32),
        compiler_params=_cparams(("arbitrary",)),
        name="even_conv_out_proj_sample",
    )(*args)


def _suffix_sums(f):
    lane = lax.broadcasted_iota(jnp.int32, f.shape, 1)
    x = f
    d = 1
    while d < PAGE_SIZE:
        x = x + jnp.where(lane + d < PAGE_SIZE, pltpu.roll(x, PAGE_SIZE - d, 1), 0.0)
        d *= 2
    return x - f


def _fox_sample_kernel(pt_ref, q_ref, kn_ref, vn_ref, lfn_ref, ck_ref, cv_ref, cf_ref, o_ref,
                       kbuf, vbuf, fbuf, sems, m_ref, l_ref, acc_ref, carry_ref, *, layer, n_heads, pages):
    db, n_pages = pt_ref.shape
    n_groups = n_pages // pages
    n_steps = db * n_groups
    d_attn = q_ref.shape[-1]
    head_of_lane = lax.broadcasted_iota(jnp.int32, (n_heads, d_attn), 1) // HEAD_DIM
    own = head_of_lane == lax.broadcasted_iota(jnp.int32, (n_heads, d_attn), 0)

    def page_copies(t, slot, p):
        b = lax.div(t, n_groups)
        j = t - b * n_groups
        page = pt_ref[b, n_pages - 1 - (j * pages + p)]
        return (pltpu.make_async_copy(ck_ref.at[layer, page], kbuf.at[slot, p], sems.at[0, slot]),
                pltpu.make_async_copy(cv_ref.at[layer, page], vbuf.at[slot, p], sems.at[1, slot]),
                pltpu.make_async_copy(cf_ref.at[layer, page], fbuf.at[slot, p], sems.at[2, slot]))

    def start_group(t, slot):
        def one(p, c):
            for cp in page_copies(t, slot, p):
                cp.start()
            return c
        lax.fori_loop(0, pages, one, 0)

    def wait_group(t, slot):
        def one(p, c):
            for cp in page_copies(t, slot, p):
                cp.wait()
            return c
        lax.fori_loop(0, pages, one, 0)

    start_group(0, 0)

    def step(t, carry_unused):
        slot = lax.rem(t, 2)
        b = lax.div(t, n_groups)
        j = t - b * n_groups

        @pl.when(t + 1 < n_steps)
        def _():
            start_group(t + 1, 1 - slot)

        q = q_ref[b].astype(f32)
        qbd32 = jnp.where(own, jnp.broadcast_to(q, (n_heads, d_attn)), 0.0)
        qbd = qbd32.astype(bf16)

        @pl.when(j == 0)
        def _():
            s_new = jnp.sum(qbd32 * kn_ref[b], axis=1, keepdims=True) - lfn_ref[b]
            m_ref[...] = s_new
            l_ref[...] = jnp.ones(l_ref.shape, f32)
            acc_ref[...] = jnp.broadcast_to(vn_ref[b], acc_ref.shape)
            carry_ref[...] = jnp.zeros(carry_ref.shape, f32)

        wait_group(t, slot)

        f_all = fbuf[slot].reshape(pages * n_heads, PAGE_SIZE)
        suffix = _suffix_sums(f_all)
        totals = jnp.sum(f_all, axis=1, keepdims=True)
        carry = carry_ref[...]
        scores = []
        for p in range(pages):
            rows = slice(p * n_heads, (p + 1) * n_heads)
            s = _dot(qbd, kbuf[slot, p].astype(bf16)) + (carry + suffix[rows])
            scores.append(s)
            carry = carry + totals[rows]
        carry_ref[...] = carry

        m_prev = m_ref[...]
        m_new = m_prev
        for s in scores:
            m_new = jnp.maximum(m_new, jnp.max(s, axis=1, keepdims=True))
        alpha = jnp.exp(m_prev - m_new)
        l_new = alpha * l_ref[...]
        acc = alpha * acc_ref[...]
        for p in range(pages):
            e = jnp.exp(scores[p] - m_new)
            l_new = l_new + jnp.sum(e, axis=1, keepdims=True)
            acc = acc + _dot_nt(e.astype(bf16), vbuf[slot, p].astype(bf16))
        m_ref[...] = m_new
        l_ref[...] = l_new
        acc_ref[...] = acc

        @pl.when(j == n_groups - 1)
        def _():
            o_ref[b] = jnp.sum(jnp.where(own, acc / l_new, 0.0), axis=0, keepdims=True)

        return carry_unused

    lax.fori_loop(0, n_steps, step, 0)


def _fox_sample(page_table, q, k_new, v_new, lf_new, cache_k, cache_v, cache_f_t, layer):
    db, n_pages = page_table.shape
    n_heads = lf_new.shape[1]
    d_attn = q.shape[1]
    pages = PAGES_PER_STEP
    assert n_pages % pages == 0
    whole = lambda shape: pl.BlockSpec(shape, lambda i, pt: (0,) * len(shape))
    hbm = pl.BlockSpec(memory_space=pl.ANY)
    kern = functools.partial(_fox_sample_kernel, layer=layer, n_heads=n_heads, pages=pages)
    out = pl.pallas_call(
        kern,
        grid_spec=pltpu.PrefetchScalarGridSpec(
            num_scalar_prefetch=1,
            grid=(1,),
            in_specs=[whole((db, 1, d_attn)), whole((db, 1, d_attn)), whole((db, 1, d_attn)),
                      whole((db, n_heads, 1)), hbm, hbm, hbm],
            out_specs=whole((db, 1, d_attn)),
            scratch_shapes=[
                pltpu.VMEM((2, pages, d_attn, PAGE_SIZE), f32),
                pltpu.VMEM((2, pages, d_attn, PAGE_SIZE), f32),
                pltpu.VMEM((2, pages, n_heads, PAGE_SIZE), f32),
                pltpu.SemaphoreType.DMA((3, 2)),
                pltpu.VMEM((n_heads, 1), f32), pltpu.VMEM((n_heads, 1), f32),
                pltpu.VMEM((n_heads, d_attn), f32), pltpu.VMEM((n_heads, 1), f32)],
        ),
        out_shape=jax.ShapeDtypeStruct((db, 1, d_attn), f32),
        compiler_params=_cparams(("arbitrary",)),
        name="fox_sample_attention",
    )(page_table, q.reshape(db, 1, d_attn), k_new.reshape(db, 1, d_attn), v_new.reshape(db, 1, d_attn),
      lf_new.reshape(db, n_heads, 1), cache_k, cache_v, cache_f_t)
    return out.reshape(db, d_attn)


def _sgu_kernel(x_ref, g_ref, win_ref, lg_ref, lb_ref, ws_ref, bs_ref, wout_ref, *out_refs, tm, sample):
    d_sgu = wout_ref.shape[0]
    x = x_ref[...]
    h = _rms(x, g_ref[...]).astype(bf16)
    u = _gelu(_dot(h, win_ref[:, 0:d_sgu]))
    v = _gelu(_dot(h, win_ref[:, d_sgu:2 * d_sgu]))
    vn = _layernorm(v, lg_ref[...], lb_ref[...])
    if sample:
        o_ref, vn_ref = out_refs
        vn_ref[...] = vn
        gated = u * (vn * ws_ref[...] + bs_ref[...])
    else:
        (o_ref,) = out_refs
        vnb = vn.astype(bf16)
        group = d_sgu // N_SGU_GROUPS
        rows = []
        for r in range(tm // CHUNK):
            rs = slice(r * CHUNK, (r + 1) * CHUNK)
            cols = [_dot(ws_ref[g], vnb[rs, g * group:(g + 1) * group]) for g in range(N_SGU_GROUPS)]
            rows.append(jnp.concatenate(cols, axis=1) + bs_ref[...])
        mixed = jnp.concatenate(rows, axis=0)
        gated = u * mixed
    o_ref[...] = x + _dot(gated.astype(bf16), wout_ref[...])


def _sgu(x, g, win, lg, lb, ws, bs, wout, *, tm, sample):
    m, d = x.shape
    d_sgu = wout.shape[0]
    kern = functools.partial(_sgu_kernel, tm=tm, sample=sample)
    out_shape = [jax.ShapeDtypeStruct((m, d), f32)]
    out_specs = [_row_spec(tm, d)]
    if sample:
        out_shape.append(jax.ShapeDtypeStruct((m, d_sgu), f32))
        out_specs.append(_row_spec(tm, d_sgu))
    outs = pl.pallas_call(
        kern,
        grid=(m // tm,),
        in_specs=[_row_spec(tm, d), _const_spec(g.shape), _const_spec(win.shape), _const_spec(lg.shape),
                  _const_spec(lb.shape), _const_spec(ws.shape), _const_spec(bs.shape), _const_spec(wout.shape)],
        out_specs=tuple(out_specs),
        out_shape=tuple(out_shape),
        compiler_params=_cparams(("parallel",)),
        name="sgu_mixer_sample" if sample else "sgu_mixer",
    )(x, g, win, lg, lb, ws, bs, wout)
    return outs if sample else outs[0]


def _ffn_kernel(x_ref, g_ref, wg_ref, wu_ref, wd_ref, gf_ref, o_ref, *, tf, final):
    x = x_ref[...]
    h = _rms(x, g_ref[...]).astype(bf16)
    acc = x
    d_ff = wg_ref.shape[1]
    for c in range(d_ff // tf):
        sl = slice(c * tf, (c + 1) * tf)
        a = _silu(_dot(h, wg_ref[:, sl])) * _dot(h, wu_ref[:, sl])
        acc = acc + _dot(a.astype(bf16), wd_ref[sl, :])
    if final:
        acc = _rms(acc, gf_ref[...])
    o_ref[...] = acc


def _ffn(x, g, wg, wu, wd, gf, *, tm, final):
    m, d = x.shape
    d_ff = wg.shape[1]
    tf = d_ff // 2 if (d_ff // 2) % LANES == 0 else d_ff
    kern = functools.partial(_ffn_kernel, tf=tf, final=final)
    return pl.pallas_call(
        kern,
        grid=(m // tm,),
        in_specs=[_row_spec(tm, d), _const_spec(g.shape), _const_spec(wg.shape), _const_spec(wu.shape),
                  _const_spec(wd.shape), _const_spec(gf.shape)],
        out_specs=_row_spec(tm, d),
        out_shape=jax.ShapeDtypeStruct((m, d), f32),
        compiler_params=_cparams(("parallel",)),
        name="swiglu_ffn_final" if final else "swiglu_ffn",
    )(x, g, wg, wu, wd, gf)


def kernel(x_prompt, x_sample, cache_k, cache_v, cache_logf, page_table, state_conv, norm_mix, norm_ffn, norm_final, w_in_even, b_forget, conv_w, conv_b, conv_ln_g, conv_ln_b, w_out_even, w_in_odd, sgu_ln_g, sgu_ln_b, sgu_w, sgu_b, w_out_odd, w_gate, w_up, w_down):
    batch, seq, d_model = x_prompt.shape
    db, dec_seq, _ = x_sample.shape
    assert dec_seq == 1, "the sample path handles one new token per sequence"
    depth = norm_mix.shape[0]
    n_heads = b_forget.shape[1]
    d_attn = n_heads * HEAD_DIM
    d_conv = conv_w.shape[2]
    n_phys = cache_k.shape[1]
    n_tok = batch * seq
    row = lambda a: a.reshape(1, -1)

    xp = x_prompt.reshape(n_tok, d_model)
    xs = x_sample.reshape(db, d_model)
    tm_p = ROW_TILE
    tm_s = db

    feature_major = lambda a: jnp.transpose(a, (0, 1, 3, 4, 2)).reshape(a.shape[0], n_phys, d_attn, PAGE_SIZE)
    cache_kt = feature_major(cache_k)
    cache_vt = feature_major(cache_v)
    cache_f_t = jnp.swapaxes(cache_logf.astype(f32), 2, 3)

    kp_l, vp_l, fp_l, cp_l = [], [], [], []
    ks_l, vs_l, fs_l, cs_l = [], [], [], []
    sgu_l = []
    for l in range(depth):
        g_mix = row(norm_mix[l])
        last = l == depth - 1
        if l % 2 == 0:
            i = l // 2
            w_in = w_in_even[i]
            w_in_t = w_in.T
            wq = w_in[:, 0:d_attn].astype(bf16)
            wkt = w_in_t[d_attn:2 * d_attn].astype(bf16)
            wvt = w_in_t[2 * d_attn:3 * d_attn].astype(bf16)
            o = 3 * d_attn
            wft = w_in_t[o:o + n_heads].astype(bf16)
            wglu = w_in[:, o + n_heads:].astype(bf16)
            wa = w_out_even[i][0:d_attn].astype(bf16)
            wc = w_out_even[i][d_attn:].astype(bf16)
            cw, cb, lg, lb = conv_w[i], row(conv_b[i]), row(conv_ln_g[i]), row(conv_ln_b[i])

            qb, ktb, vtb, kt, vt, lf_t, glu = _even_in_prompt(
                xp, g_mix, wq, wkt, wvt, wft, wglu, b_forget[i].reshape(n_heads, 1), tm=tm_p, batch=batch, seq=seq)
            c = _cumsum_lanes(lf_t)
            attn = _fox_prompt(qb, ktb, vtb, c, batch=batch, seq=seq)
            xp = _even_out(xp, attn, glu, cw, cb, lg, lb, wa, wc, tm=tm_p, seq=seq)
            token_major = lambda a: jnp.transpose(a.reshape(batch, n_heads, HEAD_DIM, seq), (0, 3, 1, 2))
            kp_l.append(token_major(kt))
            vp_l.append(token_major(vt))
            fp_l.append(jnp.swapaxes(lf_t, 1, 2))
            cp_l.append(glu.reshape(batch, seq, d_conv)[:, seq - (CONV_WIDTH - 1):])

            qb2, k2, v2, lf2, glu2 = _even_in_sample(xs, g_mix, wq, wkt, wvt, wft, wglu, row(b_forget[i]))
            attn2 = _fox_sample(page_table, qb2, k2, v2, lf2, cache_kt, cache_vt, cache_f_t, i)
            state_t = jnp.swapaxes(state_conv[i], 0, 1)
            xs = _even_out_sample(xs, attn2, glu2, state_t, cw, cb, lg, lb, wa, wc)
            ks_l.append(k2.reshape(db, 1, n_heads, HEAD_DIM))
            vs_l.append(v2.reshape(db, 1, n_heads, HEAD_DIM))
            fs_l.append(lf2.reshape(db, 1, n_heads))
            cs_l.append(jnp.concatenate([state_conv[i][:, 1:], glu2[:, None, :]], axis=1))
        else:
            j = l // 2
            d_sgu = w_out_odd.shape[1]
            group = d_sgu // N_SGU_GROUPS
            win = w_in_odd[j].astype(bf16)
            wout = w_out_odd[j].astype(bf16)
            lg, lb = row(sgu_ln_g[j]), row(sgu_ln_b[j])
            causal = jnp.tril(jnp.ones((CHUNK, CHUNK), dtype=bool))
            ws = jnp.where(causal, sgu_w[j], 0).astype(bf16)
            bs = jnp.repeat(jnp.swapaxes(sgu_b[j], 0, 1), group, axis=1)
            xp = _sgu(xp, g_mix, win, lg, lb, ws, bs, wout, tm=tm_p, sample=False)
            ws0 = row(jnp.repeat(sgu_w[j][:, 0, 0], group))
            bs0 = row(jnp.repeat(sgu_b[j][:, 0], group))
            xs, vn_s = _sgu(xs, g_mix, win, lg, lb, ws0, bs0, wout, tm=tm_s, sample=True)
            sgu_l.append(vn_s.reshape(db, 1, d_sgu))
        g_ffn = row(norm_ffn[l])
        gf = row(norm_final)
        wg, wu, wd = w_gate[l].astype(bf16), w_up[l].astype(bf16), w_down[l].astype(bf16)
        xp = _ffn(xp, g_ffn, wg, wu, wd, gf, tm=tm_p, final=last)
        xs = _ffn(xs, g_ffn, wg, wu, wd, gf, tm=tm_s, final=last)

    return (xp.reshape(batch, seq, d_model), xs.reshape(db, 1, d_model),
            jnp.stack(kp_l), jnp.stack(vp_l), jnp.stack(fp_l), jnp.stack(cp_l),
            jnp.stack(ks_l), jnp.stack(vs_l), jnp.stack(fs_l), jnp.stack(cs_l),
            jnp.stack(sgu_l))
```

```python
import functools

import jax
import jax.numpy as jnp
from jax import lax
from jax.experimental import pallas as pl
from jax.experimental.pallas import tpu as pltpu

HEAD_DIM = 64
CONV_WIDTH = 31
CHUNK = 128
N_SGU_GROUPS = 8
PAGE_SIZE = 128
RMS_EPS = 1e-6
LN_EPS = 1e-5

LANES = 128
SUBLANES = 8
VMEM_LIMIT = 56 * 1024 * 1024

ROW_TILE = 512
ATTN_TQ = 512
ATTN_TK = 512
CONV_HALO = 32
CONV_ROWS = 64
PAGES_PER_STEP = 16
NEG = -1e30

bf16 = jnp.bfloat16
f32 = jnp.float32


def _cparams(sem):
    return pltpu.CompilerParams(dimension_semantics=sem, vmem_limit_bytes=VMEM_LIMIT)


def _dot(a, b):
    return jnp.dot(a, b, preferred_element_type=f32)


def _dot_nt(a, b):
    return lax.dot_general(a, b, (((1,), (1,)), ((), ())), preferred_element_type=f32)


def _rms(x, g):
    ms = jnp.mean(x * x, axis=-1, keepdims=True)
    return x * lax.rsqrt(ms + RMS_EPS) * g


def _layernorm(x, g, b):
    mu = jnp.mean(x, axis=-1, keepdims=True)
    xc = x - mu
    var = jnp.mean(xc * xc, axis=-1, keepdims=True)
    return xc * lax.rsqrt(var + LN_EPS) * g + b


def _silu(x):
    return x * jax.nn.sigmoid(x)


def _gelu(x):
    return 0.5 * x * (1.0 + lax.erf(x * (2.0 ** -0.5)))


def _log_sigmoid(z):
    return jnp.minimum(z, 0.0) - jnp.log1p(jnp.exp(-jnp.abs(z)))


def _const_spec(shape):
    nd = len(shape)
    return pl.BlockSpec(shape, lambda *_: (0,) * nd)


def _row_spec(tm, width):
    return pl.BlockSpec((tm, width), lambda i: (i, 0))


def _even_in_prompt_kernel(x_ref, g_ref, wq_ref, wkt_ref, wvt_ref, wft_ref, wglu_ref, bf_ref,
                           q_ref, ktb_ref, vtb_ref, kt_ref, vt_ref, lft_ref, glu_ref, *, d_conv):
    h = _rms(x_ref[...], g_ref[...]).astype(bf16)
    q_ref[...] = (_dot(h, wq_ref[...]) * (HEAD_DIM ** -0.5)).astype(bf16)
    kt = _dot_nt(wkt_ref[...], h)
    kt_ref[...] = kt
    ktb_ref[...] = kt.astype(bf16)
    vt = _dot_nt(wvt_ref[...], h)
    vt_ref[...] = vt
    vtb_ref[...] = vt.astype(bf16)
    lft_ref[...] = _log_sigmoid(_dot_nt(wft_ref[...], h) + bf_ref[...])
    a = _dot(h, wglu_ref[:, 0:d_conv])
    b = _dot(h, wglu_ref[:, d_conv:2 * d_conv])
    glu_ref[...] = a * jax.nn.sigmoid(b)


def _even_in_prompt(x, g, wq, wkt, wvt, wft, wglu, bf_col, *, tm, batch, seq):
    m, d = x.shape
    d_attn = wq.shape[1]
    n_heads = wft.shape[0]
    d_conv = wglu.shape[1] // 2
    tiles_per_seq = seq // tm
    kern = functools.partial(_even_in_prompt_kernel, d_conv=d_conv)
    t_spec = lambda rows: pl.BlockSpec((None, rows, tm), lambda i: (i // tiles_per_seq, 0, i % tiles_per_seq))
    out_shape = (
        jax.ShapeDtypeStruct((m, d_attn), bf16),
        jax.ShapeDtypeStruct((batch, d_attn, seq), bf16),
        jax.ShapeDtypeStruct((batch, d_attn, seq), bf16),
        jax.ShapeDtypeStruct((batch, d_attn, seq), f32),
        jax.ShapeDtypeStruct((batch, d_attn, seq), f32),
        jax.ShapeDtypeStruct((batch, n_heads, seq), f32),
        jax.ShapeDtypeStruct((m, d_conv), f32),
    )
    args = (x, g, wq, wkt, wvt, wft, wglu, bf_col)
    return pl.pallas_call(
        kern,
        grid=(m // tm,),
        in_specs=[_row_spec(tm, d)] + [_const_spec(a.shape) for a in args[1:]],
        out_specs=(_row_spec(tm, d_attn), t_spec(d_attn), t_spec(d_attn), t_spec(d_attn), t_spec(d_attn),
                   t_spec(n_heads), _row_spec(tm, d_conv)),
        out_shape=out_shape,
        compiler_params=_cparams(("parallel",)),
        name="even_in_proj",
    )(*args)


def _even_in_sample_kernel(x_ref, g_ref, wq_ref, wkt_ref, wvt_ref, wft_ref, wglu_ref, bf_ref,
                           q_ref, k_ref, v_ref, lf_ref, glu_ref, *, d_conv):
    h = _rms(x_ref[...], g_ref[...]).astype(bf16)
    q_ref[...] = (_dot(h, wq_ref[...]) * (HEAD_DIM ** -0.5)).astype(bf16)
    k_ref[...] = _dot_nt(h, wkt_ref[...])
    v_ref[...] = _dot_nt(h, wvt_ref[...])
    lf_ref[...] = _log_sigmoid(_dot_nt(h, wft_ref[...]) + bf_ref[...])
    a = _dot(h, wglu_ref[:, 0:d_conv])
    b = _dot(h, wglu_ref[:, d_conv:2 * d_conv])
    glu_ref[...] = a * jax.nn.sigmoid(b)


def _even_in_sample(x, g, wq, wkt, wvt, wft, wglu, bf_row):
    m, d = x.shape
    d_attn = wq.shape[1]
    n_heads = wft.shape[0]
    d_conv = wglu.shape[1] // 2
    kern = functools.partial(_even_in_sample_kernel, d_conv=d_conv)
    out_shape = (
        jax.ShapeDtypeStruct((m, d_attn), bf16),
        jax.ShapeDtypeStruct((m, d_attn), f32),
        jax.ShapeDtypeStruct((m, d_attn), f32),
        jax.ShapeDtypeStruct((m, n_heads), f32),
        jax.ShapeDtypeStruct((m, d_conv), f32),
    )
    args = (x, g, wq, wkt, wvt, wft, wglu, bf_row)
    return pl.pallas_call(
        kern,
        grid=(1,),
        in_specs=[_const_spec(a.shape) for a in args],
        out_specs=tuple(_const_spec(o.shape) for o in out_shape),
        out_shape=out_shape,
        compiler_params=_cparams(("arbitrary",)),
        name="even_in_proj_sample",
    )(*args)


def _cumsum_kernel(f_ref, c_ref):
    x = f_ref[...]
    n = x.shape[-1]
    lane = lax.broadcasted_iota(jnp.int32, x.shape, 1)
    d = 1
    while d < n:
        x = x + jnp.where(lane >= d, pltpu.roll(x, d, 1), 0.0)
        d *= 2
    c_ref[...] = x


def _cumsum_lanes(f_t):
    b, h, s = f_t.shape
    return pl.pallas_call(
        _cumsum_kernel,
        grid=(b,),
        in_specs=[pl.BlockSpec((None, h, s), lambda i: (i, 0, 0))],
        out_specs=pl.BlockSpec((None, h, s), lambda i: (i, 0, 0)),
        out_shape=jax.ShapeDtypeStruct((b, h, s), f32),
        compiler_params=_cparams(("parallel",)),
        name="logf_cumsum",
    )(f_t)


def _fox_prompt_kernel(q_ref, k_ref, v_ref, c_ref, o_ref, m_ref, acc_ref, *, tq, tk, n_heads):
    qi = pl.program_id(1)
    pair = 2 * HEAD_DIM
    lane = lax.broadcasted_iota(jnp.int32, (tq, pair), 1)
    first = lane < HEAD_DIM
    vrow = lax.broadcasted_iota(jnp.int32, (pair, tk), 0)
    sum_lane = (HEAD_DIM, 0)
    own_rows = (vrow < HEAD_DIM, vrow >= HEAD_DIM)
    keep = [jnp.where(own_rows[i], 1.0, 0.0).astype(bf16) for i in range(2)]
    ones_row = [jnp.where(vrow == sum_lane[i], 1.0, 0.0).astype(bf16) for i in range(2)]

    qm = []
    for h in range(n_heads):
        q = q_ref[:, (h // 2) * pair:(h // 2 + 1) * pair]
        zero = jnp.zeros_like(q)
        qm.append(jnp.where(first, q, zero) if h % 2 == 0 else jnp.where(first, zero, q))

    m_ref[...] = jnp.full(m_ref.shape, NEG, f32)
    acc_ref[...] = jnp.zeros(acc_ref.shape, f32)

    def block(j, masked):
        start = pl.multiple_of(j * tk, tk)
        for h in range(n_heads):
            rows = slice((h // 2) * pair, (h // 2 + 1) * pair)
            kb = k_ref[rows, pl.ds(start, tk)]
            vb = v_ref[rows, pl.ds(start, tk)]
            va = vb * keep[h % 2] + ones_row[h % 2]
            s = _dot(qm[h], kb) - c_ref[h:h + 1, pl.ds(start, tk)]
            if masked:
                row = lax.broadcasted_iota(jnp.int32, (tq, tk), 0)
                col = lax.broadcasted_iota(jnp.int32, (tq, tk), 1)
                s = jnp.where(col <= row, s, NEG)
            m_prev = m_ref[h]
            m_new = jnp.maximum(m_prev, jnp.max(s, axis=1, keepdims=True))
            p = jnp.exp(s - jnp.concatenate([m_new] * (tk // pair), axis=1))
            m_ref[h] = m_new
            acc_ref[h] = jnp.exp(m_prev - m_new) * acc_ref[h] + _dot_nt(p.astype(bf16), va)

    def body(j, carry):
        block(j, False)
        return carry

    n_full = qi * (tq // tk)
    lax.fori_loop(0, n_full, body, 0)
    block(n_full, True)
    for h in range(0, n_heads, 2):
        a0, a1 = acc_ref[h], acc_ref[h + 1]
        o0 = a0 / a0[:, sum_lane[0]:sum_lane[0] + 1]
        o1 = a1 / a1[:, sum_lane[1]:sum_lane[1] + 1]
        o_ref[:, (h // 2) * pair:(h // 2 + 1) * pair] = jnp.where(first, o0, o1).astype(o_ref.dtype)


def _fox_prompt(q, kt, vt, c, *, batch, seq):
    n, d_attn = q.shape
    n_heads = c.shape[1]
    tq, tk = ATTN_TQ, ATTN_TK
    assert tq == tk and seq % tq == 0
    assert 2 * HEAD_DIM == LANES and n_heads % 2 == 0, "one 128-wide block holds exactly two heads"
    nq = seq // tq
    kern = functools.partial(_fox_prompt_kernel, tq=tq, tk=tk, n_heads=n_heads)
    return pl.pallas_call(
        kern,
        grid=(batch, nq),
        in_specs=[
            pl.BlockSpec((tq, d_attn), lambda b, i: (b * nq + i, 0)),
            pl.BlockSpec((None, d_attn, seq), lambda b, i: (b, 0, 0)),
            pl.BlockSpec((None, d_attn, seq), lambda b, i: (b, 0, 0)),
            pl.BlockSpec((None, n_heads, seq), lambda b, i: (b, 0, 0)),
        ],
        out_specs=pl.BlockSpec((tq, d_attn), lambda b, i: (b * nq + i, 0)),
        out_shape=jax.ShapeDtypeStruct((n, d_attn), bf16),
        scratch_shapes=[pltpu.VMEM((n_heads, tq, LANES), f32), pltpu.VMEM((n_heads, tq, LANES), f32)],
        compiler_params=_cparams(("parallel", "arbitrary")),
        name="fox_prompt_attention",
    )(q, kt, vt, c)


def _conv_ln_silu(y, cb, lg, lb):
    return _silu(_layernorm(y + cb, lg, lb))


def _even_out_kernel(x_ref, attn_ref, glu_ref, halo_ref, cw_ref, cb_ref, lg_ref, lb_ref, wa_ref, wc_ref,
                     o_ref, ext_ref, sh_ref, conv_ref, *, tm, tiles_per_seq):
    i = pl.program_id(0)
    starts_sequence = (i % tiles_per_seq) == 0
    ext_ref[0:CONV_HALO, :] = jnp.where(starts_sequence, 0.0, halo_ref[...])
    ext_ref[CONV_HALO:CONV_HALO + tm, :] = glu_ref[...]
    d_conv = glu_ref.shape[1]
    off = CONV_HALO - (CONV_WIDTH - 1)
    sh_rows = sh_ref.shape[1]
    for b in range(1, SUBLANES):
        sh_ref[b - 1] = ext_ref[b:b + sh_rows, :]
    for c in range(d_conv // LANES):
        cs = slice(c * LANES, (c + 1) * LANES)
        for r in range(tm // CONV_ROWS):
            acc = jnp.zeros((CONV_ROWS, LANES), f32)
            for t in range(CONV_WIDTH):
                a, b = divmod(off + t, SUBLANES)
                lo = r * CONV_ROWS + a * SUBLANES
                src = ext_ref[lo:lo + CONV_ROWS, cs] if b == 0 else sh_ref[b - 1, lo:lo + CONV_ROWS, cs]
                acc = acc + cw_ref[t:t + 1, cs] * src
            conv_ref[r * CONV_ROWS:(r + 1) * CONV_ROWS, cs] = acc
    conv = _conv_ln_silu(conv_ref[...], cb_ref[...], lg_ref[...], lb_ref[...]).astype(bf16)
    o_ref[...] = x_ref[...] + _dot(attn_ref[...], wa_ref[...]) + _dot(conv, wc_ref[...])


def _even_out(x, attn, glu, cw, cb, lg, lb, wa, wc, *, tm, seq):
    m, d = x.shape
    d_attn = attn.shape[1]
    d_conv = glu.shape[1]
    assert seq % tm == 0 and tm % CONV_HALO == 0
    tiles_per_seq = seq // tm
    halo_per_tile = tm // CONV_HALO
    kern = functools.partial(_even_out_kernel, tm=tm, tiles_per_seq=tiles_per_seq)
    return pl.pallas_call(
        kern,
        grid=(m // tm,),
        in_specs=[
            _row_spec(tm, d), _row_spec(tm, d_attn), _row_spec(tm, d_conv),
            pl.BlockSpec((CONV_HALO, d_conv), lambda i: (jnp.maximum(i * halo_per_tile - 1, 0), 0)),
            _const_spec(cw.shape), _const_spec(cb.shape), _const_spec(lg.shape), _const_spec(lb.shape),
            _const_spec(wa.shape), _const_spec(wc.shape),
        ],
        out_specs=_row_spec(tm, d),
        out_shape=jax.ShapeDtypeStruct((m, d), f32),
        scratch_shapes=[pltpu.VMEM((CONV_HALO + tm, d_conv), f32),
                        pltpu.VMEM((SUBLANES - 1, CONV_HALO + tm - SUBLANES, d_conv), f32),
                        pltpu.VMEM((tm, d_conv), f32)],
        compiler_params=_cparams(("parallel",)),
        name="even_conv_out_proj",
    )(x, attn, glu, glu, cw, cb, lg, lb, wa, wc)


def _even_out_sample_kernel(x_ref, attn_ref, glu_ref, st_ref, cw_ref, cb_ref, lg_ref, lb_ref, wa_ref, wc_ref,
                            o_ref):
    y = cw_ref[CONV_WIDTH - 1:CONV_WIDTH, :] * glu_ref[...]
    for t in range(CONV_WIDTH - 1):
        y = y + cw_ref[t:t + 1, :] * st_ref[t]
    conv = _conv_ln_silu(y, cb_ref[...], lg_ref[...], lb_ref[...]).astype(bf16)
    o_ref[...] = (x_ref[...] + _dot(attn_ref[...].astype(bf16), wa_ref[...]) + _dot(conv, wc_ref[...]))


def _even_out_sample(x, attn, glu, state_t, cw, cb, lg, lb, wa, wc):
    m, d = x.shape
    args = (x, attn, glu, state_t, cw, cb, lg, lb, wa, wc)
    return pl.pallas_call(
        _even_out_sample_kernel,
        grid=(1,),
        in_specs=[_const_spec(a.shape) for a in args],
        out_specs=_const_spec((m, d)),
        out_shape=jax.ShapeDtypeStruct((m, d), f32),
        compiler_params=_cparams(("arbitrary",)),
        name="even_conv_out_proj_sample",
    )(*args)


def _suffix_sums(f):
    lane = lax.broadcasted_iota(jnp.int32, f.shape, 1)
    x = f
    d = 1
    while d < PAGE_SIZE:
        x = x + jnp.where(lane + d < PAGE_SIZE, pltpu.roll(x, PAGE_SIZE - d, 1), 0.0)
        d *= 2
    return x - f


def _fox_sample_scores_kernel(pt_ref, q_ref, kn_ref, lfn_ref, ind_ref, ck_ref, cf_ref,
                              e_ref, w_ref, l_ref, fl_ref, kbuf, fbuf, sems, s_ref, carry_ref,
                              *, layer, n_heads, pages):
    db, n_pages = pt_ref.shape
    n_groups = n_pages // pages
    n_steps = db * n_groups
    d_attn = q_ref.shape[-1]
    head_of_lane = lax.broadcasted_iota(jnp.int32, (n_heads, d_attn), 1) // HEAD_DIM
    own = head_of_lane == lax.broadcasted_iota(jnp.int32, (n_heads, d_attn), 0)

    def page_copies(t, slot, p):
        b = lax.div(t, n_groups)
        j = t - b * n_groups
        page = pt_ref[b, n_pages - 1 - (j * pages + p)]
        return (pltpu.make_async_copy(ck_ref.at[layer, page], kbuf.at[slot, p], sems.at[0, slot]),
                pltpu.make_async_copy(cf_ref.at[layer, page], fbuf.at[slot, p], sems.at[1, slot]))

    def start_group(t, slot):
        def one(p, c):
            for cp in page_copies(t, slot, p):
                cp.start()
            return c
        lax.fori_loop(0, pages, one, 0)

    def wait_group(t, slot):
        def one(p, c):
            for cp in page_copies(t, slot, p):
                cp.wait()
            return c
        lax.fori_loop(0, pages, one, 0)

    start_group(0, 0)

    def step(t, carry_unused):
        slot = lax.rem(t, 2)
        b = lax.div(t, n_groups)
        j = t - b * n_groups

        @pl.when(t + 1 < n_steps)
        def _():
            start_group(t + 1, 1 - slot)

        q = q_ref[b].astype(f32)
        qbd32 = jnp.where(own, jnp.broadcast_to(q, (n_heads, d_attn)), 0.0)
        qbd = qbd32.astype(bf16)

        @pl.when(j == 0)
        def _():
            carry_ref[...] = jnp.zeros(carry_ref.shape, f32)

        wait_group(t, slot)

        f_all = fbuf[slot].reshape(pages * n_heads, PAGE_SIZE)
        suffix = _suffix_sums(f_all)
        totals = jnp.sum(f_all, axis=1, keepdims=True)
        carry = carry_ref[...]
        for p in range(pages):
            rows = slice(p * n_heads, (p + 1) * n_heads)
            s = _dot(qbd, kbuf[slot, p].astype(bf16)) + (carry + suffix[rows])
            logical = n_pages - 1 - (j * pages + p)
            s_ref[:, pl.ds(pl.multiple_of(logical * PAGE_SIZE, PAGE_SIZE), PAGE_SIZE)] = s
            carry = carry + totals[rows]
        carry_ref[...] = carry

        @pl.when(j == n_groups - 1)
        def _():
            s_new = jnp.sum(qbd32 * kn_ref[b], axis=1, keepdims=True) - lfn_ref[b]
            s_all = s_ref[...]
            m = jnp.maximum(jnp.max(s_all, axis=1, keepdims=True), s_new)
            e = jnp.exp(s_all - m)
            w = jnp.exp(s_new - m)
            e_ref[b] = e
            w_ref[b] = w
            l_ref[b] = jnp.sum(e, axis=1, keepdims=True) + w
            per_page = _dot(e.astype(bf16), ind_ref[...])
            fl_ref[b] = (jnp.max(per_page, axis=0, keepdims=True) > 0.0).astype(jnp.int32)

        return carry_unused

    lax.fori_loop(0, n_steps, step, 0)


def _fox_sample_pv_kernel(pt_ref, fl_ref, e_ref, w_ref, l_ref, vn_ref, cv_ref, o_ref, vbuf, sems, acc_ref,
                          *, layer, n_heads, pages):
    db, n_pages = pt_ref.shape
    n_groups = n_pages // pages
    n_steps = db * n_groups
    d_attn = vn_ref.shape[-1]
    head_of_lane = lax.broadcasted_iota(jnp.int32, (n_heads, d_attn), 1) // HEAD_DIM
    own = head_of_lane == lax.broadcasted_iota(jnp.int32, (n_heads, d_attn), 0)

    def page_copy(b, logical, slot, p):
        return pltpu.make_async_copy(cv_ref.at[layer, pt_ref[b, logical]], vbuf.at[slot, p], sems.at[slot, p])

    def start_group(t, slot):
        b = lax.div(t, n_groups)
        j = t - b * n_groups

        def one(p, c):
            logical = j * pages + p

            @pl.when(fl_ref[b, logical] != 0)
            def _():
                page_copy(b, logical, slot, p).start()
            return c
        lax.fori_loop(0, pages, one, 0)

    start_group(0, 0)

    def step(t, carry_unused):
        slot = lax.rem(t, 2)
        b = lax.div(t, n_groups)
        j = t - b * n_groups

        @pl.when(t + 1 < n_steps)
        def _():
            start_group(t + 1, 1 - slot)

        @pl.when(j == 0)
        def _():
            acc_ref[...] = w_ref[b] * jnp.broadcast_to(vn_ref[b], acc_ref.shape)

        def one(p, c):
            logical = j * pages + p

            @pl.when(fl_ref[b, logical] != 0)
            def _():
                page_copy(b, logical, slot, p).wait()
                e = e_ref[b, :, pl.ds(pl.multiple_of(logical * PAGE_SIZE, PAGE_SIZE), PAGE_SIZE)]
                acc_ref[...] += _dot_nt(e.astype(bf16), vbuf[slot, p].astype(bf16))
            return c
        lax.fori_loop(0, pages, one, 0)

        @pl.when(j == n_groups - 1)
        def _():
            o_ref[b] = jnp.sum(jnp.where(own, acc_ref[...] / l_ref[b], 0.0), axis=0, keepdims=True)

        return carry_unused

    lax.fori_loop(0, n_steps, step, 0)


def _fox_sample(page_table, q, k_new, v_new, lf_new, cache_k, cache_v, cache_f_t, layer):
    db, n_pages = page_table.shape
    n_heads = lf_new.shape[1]
    d_attn = q.shape[1]
    n_past = n_pages * PAGE_SIZE
    pages = PAGES_PER_STEP
    assert n_pages % pages == 0
    hbm = pl.BlockSpec(memory_space=pl.ANY)
    q3, kn3, vn3 = q.reshape(db, 1, d_attn), k_new.reshape(db, 1, d_attn), v_new.reshape(db, 1, d_attn)
    lfn3 = lf_new.reshape(db, n_heads, 1)
    ind = (lax.broadcasted_iota(jnp.int32, (n_past, n_pages), 0) // PAGE_SIZE
           == lax.broadcasted_iota(jnp.int32, (n_past, n_pages), 1)).astype(bf16)

    whole1 = lambda shape: pl.BlockSpec(shape, lambda i, pt: (0,) * len(shape))
    e, w, l, flags = pl.pallas_call(
        functools.partial(_fox_sample_scores_kernel, layer=layer, n_heads=n_heads, pages=pages),
        grid_spec=pltpu.PrefetchScalarGridSpec(
            num_scalar_prefetch=1,
            grid=(1,),
            in_specs=[whole1(q3.shape), whole1(kn3.shape), whole1(lfn3.shape), whole1(ind.shape), hbm, hbm],
            out_specs=(whole1((db, n_heads, n_past)), whole1((db, n_heads, 1)), whole1((db, n_heads, 1)),
                       whole1((db, 1, n_pages))),
            scratch_shapes=[
                pltpu.VMEM((2, pages, d_attn, PAGE_SIZE), f32),
                pltpu.VMEM((2, pages, n_heads, PAGE_SIZE), f32),
                pltpu.SemaphoreType.DMA((2, 2)),
                pltpu.VMEM((n_heads, n_past), f32),
                pltpu.VMEM((n_heads, 1), f32)],
        ),
        out_shape=(jax.ShapeDtypeStruct((db, n_heads, n_past), f32), jax.ShapeDtypeStruct((db, n_heads, 1), f32),
                   jax.ShapeDtypeStruct((db, n_heads, 1), f32), jax.ShapeDtypeStruct((db, 1, n_pages), jnp.int32)),
        compiler_params=_cparams(("arbitrary",)),
        name="fox_sample_scores",
    )(page_table, q3, kn3, lfn3, ind, cache_k, cache_f_t)

    whole2 = lambda shape: pl.BlockSpec(shape, lambda i, pt, fl: (0,) * len(shape))
    out = pl.pallas_call(
        functools.partial(_fox_sample_pv_kernel, layer=layer, n_heads=n_heads, pages=pages),
        grid_spec=pltpu.PrefetchScalarGridSpec(
            num_scalar_prefetch=2,
            grid=(1,),
            in_specs=[whole2(e.shape), whole2(w.shape), whole2(l.shape), whole2(vn3.shape), hbm],
            out_specs=whole2((db, 1, d_attn)),
            scratch_shapes=[
                pltpu.VMEM((2, pages, d_attn, PAGE_SIZE), f32),
                pltpu.SemaphoreType.DMA((2, pages)),
                pltpu.VMEM((n_heads, d_attn), f32)],
        ),
        out_shape=jax.ShapeDtypeStruct((db, 1, d_attn), f32),
        compiler_params=_cparams(("arbitrary",)),
        name="fox_sample_pv",
    )(page_table, flags.reshape(db, n_pages), e, w, l, vn3, cache_v)
    return out.reshape(db, d_attn)


def _sgu_kernel(x_ref, g_ref, win_ref, lg_ref, lb_ref, ws_ref, bs_ref, wout_ref, *out_refs, tm, sample):
    d_sgu = wout_ref.shape[0]
    x = x_ref[...]
    h = _rms(x, g_ref[...]).astype(bf16)
    u = _gelu(_dot(h, win_ref[:, 0:d_sgu]))
    v = _gelu(_dot(h, win_ref[:, d_sgu:2 * d_sgu]))
    vn = _layernorm(v, lg_ref[...], lb_ref[...])
    if sample:
        o_ref, vn_ref = out_refs
        vn_ref[...] = vn
        gated = u * (vn * ws_ref[...] + bs_ref[...])
    else:
        (o_ref,) = out_refs
        vnb = vn.astype(bf16)
        group = d_sgu // N_SGU_GROUPS
        rows = []
        for r in range(tm // CHUNK):
            rs = slice(r * CHUNK, (r + 1) * CHUNK)
            cols = [_dot(ws_ref[g], vnb[rs, g * group:(g + 1) * group]) for g in range(N_SGU_GROUPS)]
            rows.append(jnp.concatenate(cols, axis=1) + bs_ref[...])
        mixed = jnp.concatenate(rows, axis=0)
        gated = u * mixed
    o_ref[...] = x + _dot(gated.astype(bf16), wout_ref[...])


def _sgu(x, g, win, lg, lb, ws, bs, wout, *, tm, sample):
    m, d = x.shape
    d_sgu = wout.shape[0]
    kern = functools.partial(_sgu_kernel, tm=tm, sample=sample)
    out_shape = [jax.ShapeDtypeStruct((m, d), f32)]
    out_specs = [_row_spec(tm, d)]
    if sample:
        out_shape.append(jax.ShapeDtypeStruct((m, d_sgu), f32))
        out_specs.append(_row_spec(tm, d_sgu))
    outs = pl.pallas_call(
        kern,
        grid=(m // tm,),
        in_specs=[_row_spec(tm, d), _const_spec(g.shape), _const_spec(win.shape), _const_spec(lg.shape),
                  _const_spec(lb.shape), _const_spec(ws.shape), _const_spec(bs.shape), _const_spec(wout.shape)],
        out_specs=tuple(out_specs),
        out_shape=tuple(out_shape),
        compiler_params=_cparams(("parallel",)),
        name="sgu_mixer_sample" if sample else "sgu_mixer",
    )(x, g, win, lg, lb, ws, bs, wout)
    return outs if sample else outs[0]


def _ffn_kernel(x_ref, g_ref, wg_ref, wu_ref, wd_ref, gf_ref, o_ref, *, tf, final):
    x = x_ref[...]
    h = _rms(x, g_ref[...]).astype(bf16)
    acc = x
    d_ff = wg_ref.shape[1]
    for c in range(d_ff // tf):
        sl = slice(c * tf, (c + 1) * tf)
        a = _silu(_dot(h, wg_ref[:, sl])) * _dot(h, wu_ref[:, sl])
        acc = acc + _dot(a.astype(bf16), wd_ref[sl, :])
    if final:
        acc = _rms(acc, gf_ref[...])
    o_ref[...] = acc


def _ffn(x, g, wg, wu, wd, gf, *, tm, final):
    m, d = x.shape
    d_ff = wg.shape[1]
    tf = d_ff // 2 if (d_ff // 2) % LANES == 0 else d_ff
    kern = functools.partial(_ffn_kernel, tf=tf, final=final)
    return pl.pallas_call(
        kern,
        grid=(m // tm,),
        in_specs=[_row_spec(tm, d), _const_spec(g.shape), _const_spec(wg.shape), _const_spec(wu.shape),
                  _const_spec(wd.shape), _const_spec(gf.shape)],
        out_specs=_row_spec(tm, d),
        out_shape=jax.ShapeDtypeStruct((m, d), f32),
        compiler_params=_cparams(("parallel",)),
        name="swiglu_ffn_final" if final else "swiglu_ffn",
    )(x, g, wg, wu, wd, gf)


def kernel(x_prompt, x_sample, cache_k, cache_v, cache_logf, page_table, state_conv, norm_mix, norm_ffn, norm_final, w_in_even, b_forget, conv_w, conv_b, conv_ln_g, conv_ln_b, w_out_even, w_in_odd, sgu_ln_g, sgu_ln_b, sgu_w, sgu_b, w_out_odd, w_gate, w_up, w_down):
    batch, seq, d_model = x_prompt.shape
    db, dec_seq, _ = x_sample.shape
    assert dec_seq == 1, "the sample path handles one new token per sequence"
    depth = norm_mix.shape[0]
    n_heads = b_forget.shape[1]
    d_attn = n_heads * HEAD_DIM
    d_conv = conv_w.shape[2]
    n_phys = cache_k.shape[1]
    n_tok = batch * seq
    row = lambda a: a.reshape(1, -1)

    xp = x_prompt.reshape(n_tok, d_model)
    xs = x_sample.reshape(db, d_model)
    tm_p = ROW_TILE
    tm_s = db

    feature_major = lambda a: jnp.transpose(a, (0, 1, 3, 4, 2)).reshape(a.shape[0], n_phys, d_attn, PAGE_SIZE)
    cache_kt = feature_major(cache_k)
    cache_vt = feature_major(cache_v)
    cache_f_t = jnp.swapaxes(cache_logf.astype(f32), 2, 3)

    kp_l, vp_l, fp_l, cp_l = [], [], [], []
    ks_l, vs_l, fs_l, cs_l = [], [], [], []
    sgu_l = []
    for l in range(depth):
        g_mix = row(norm_mix[l])
        last = l == depth - 1
        if l % 2 == 0:
            i = l // 2
            w_in = w_in_even[i]
            w_in_t = w_in.T
            wq = w_in[:, 0:d_attn].astype(bf16)
            wkt = w_in_t[d_attn:2 * d_attn].astype(bf16)
            wvt = w_in_t[2 * d_attn:3 * d_attn].astype(bf16)
            o = 3 * d_attn
            wft = w_in_t[o:o + n_heads].astype(bf16)
            wglu = w_in[:, o + n_heads:].astype(bf16)
            wa = w_out_even[i][0:d_attn].astype(bf16)
            wc = w_out_even[i][d_attn:].astype(bf16)
            cw, cb, lg, lb = conv_w[i], row(conv_b[i]), row(conv_ln_g[i]), row(conv_ln_b[i])

            qb, ktb, vtb, kt, vt, lf_t, glu = _even_in_prompt(
                xp, g_mix, wq, wkt, wvt, wft, wglu, b_forget[i].reshape(n_heads, 1), tm=tm_p, batch=batch, seq=seq)
            c = _cumsum_lanes(lf_t)
            attn = _fox_prompt(qb, ktb, vtb, c, batch=batch, seq=seq)
            xp = _even_out(xp, attn, glu, cw, cb, lg, lb, wa, wc, tm=tm_p, seq=seq)
            token_major = lambda a: jnp.transpose(a.reshape(batch, n_heads, HEAD_DIM, seq), (0, 3, 1, 2))
            kp_l.append(token_major(kt))
            vp_l.append(token_major(vt))
            fp_l.append(jnp.swapaxes(lf_t, 1, 2))
            cp_l.append(glu.reshape(batch, seq, d_conv)[:, seq - (CONV_WIDTH - 1):])

            qb2, k2, v2, lf2, glu2 = _even_in_sample(xs, g_mix, wq, wkt, wvt, wft, wglu, row(b_forget[i]))
            attn2 = _fox_sample(page_table, qb2, k2, v2, lf2, cache_kt, cache_vt, cache_f_t, i)
            state_t = jnp.swapaxes(state_conv[i], 0, 1)
            xs = _even_out_sample(xs, attn2, glu2, state_t, cw, cb, lg, lb, wa, wc)
            ks_l.append(k2.reshape(db, 1, n_heads, HEAD_DIM))
            vs_l.append(v2.reshape(db, 1, n_heads, HEAD_DIM))
            fs_l.append(lf2.reshape(db, 1, n_heads))
            cs_l.append(jnp.concatenate([state_conv[i][:, 1:], glu2[:, None, :]], axis=1))
        else:
            j = l // 2
            d_sgu = w_out_odd.shape[1]
            group = d_sgu // N_SGU_GROUPS
            win = w_in_odd[j].astype(bf16)
            wout = w_out_odd[j].astype(bf16)
            lg, lb = row(sgu_ln_g[j]), row(sgu_ln_b[j])
            causal = jnp.tril(jnp.ones((CHUNK, CHUNK), dtype=bool))
            ws = jnp.where(causal, sgu_w[j], 0).astype(bf16)
            bs = jnp.repeat(jnp.swapaxes(sgu_b[j], 0, 1), group, axis=1)
            xp = _sgu(xp, g_mix, win, lg, lb, ws, bs, wout, tm=tm_p, sample=False)
            ws0 = row(jnp.repeat(sgu_w[j][:, 0, 0], group))
            bs0 = row(jnp.repeat(sgu_b[j][:, 0], group))
            xs, vn_s = _sgu(xs, g_mix, win, lg, lb, ws0, bs0, wout, tm=tm_s, sample=True)
            sgu_l.append(vn_s.reshape(db, 1, d_sgu))
        g_ffn = row(norm_ffn[l])
        gf = row(norm_final)
        wg, wu, wd = w_gate[l].astype(bf16), w_up[l].astype(bf16), w_down[l].astype(bf16)
        xp = _ffn(xp, g_ffn, wg, wu, wd, gf, tm=tm_p, final=last)
        xs = _ffn(xs, g_ffn, wg, wu, wd, gf, tm=tm_s, final=last)

    return (xp.reshape(batch, seq, d_model), xs.reshape(db, 1, d_model),
            jnp.stack(kp_l), jnp.stack(vp_l), jnp.stack(fp_l), jnp.stack(cp_l),
            jnp.stack(ks_l), jnp.stack(vs_l), jnp.stack(fs_l), jnp.stack(cs_l),
            jnp.stack(sgu_l))
```

```python
import functools

import jax
import jax.numpy as jnp
from jax import lax
from jax.experimental import pallas as pl
from jax.experimental.pallas import tpu as pltpu

HEAD_DIM = 64
CONV_WIDTH = 31
CHUNK = 128
N_SGU_GROUPS = 8
PAGE_SIZE = 128
RMS_EPS = 1e-6
LN_EPS = 1e-5

LANES = 128
SUBLANES = 8
VMEM_LIMIT = 56 * 1024 * 1024

ROW_TILE = 512
FFN_ROW_TILE = 1024
ATTN_TQ = 512
ATTN_TK = 512
CONV_HALO = 32
CONV_ROWS = 64
PAGES_PER_STEP = 32
PV_PAGES_PER_STEP = 8
NEG = -1e30

bf16 = jnp.bfloat16
f32 = jnp.float32


def _cparams(sem):
    return pltpu.CompilerParams(dimension_semantics=sem, vmem_limit_bytes=VMEM_LIMIT)


def _dot(a, b):
    return jnp.dot(a, b, preferred_element_type=f32)


def _dot_nt(a, b):
    return lax.dot_general(a, b, (((1,), (1,)), ((), ())), preferred_element_type=f32)


def _rms(x, g):
    ms = jnp.mean(x * x, axis=-1, keepdims=True)
    return x * lax.rsqrt(ms + RMS_EPS) * g


def _layernorm(x, g, b):
    mu = jnp.mean(x, axis=-1, keepdims=True)
    xc = x - mu
    var = jnp.mean(xc * xc, axis=-1, keepdims=True)
    return xc * lax.rsqrt(var + LN_EPS) * g + b


def _silu(x):
    return x * jax.nn.sigmoid(x)


def _gelu(x):
    return 0.5 * x * (1.0 + lax.erf(x * (2.0 ** -0.5)))


def _log_sigmoid(z):
    return jnp.minimum(z, 0.0) - jnp.log1p(jnp.exp(-jnp.abs(z)))


def _const_spec(shape):
    nd = len(shape)
    return pl.BlockSpec(shape, lambda *_: (0,) * nd, pipeline_mode=pl.Buffered(1))


def _row_spec(tm, width):
    return pl.BlockSpec((tm, width), lambda i: (i, 0))


def _even_in_prompt_kernel(x_ref, g_ref, wq_ref, wkt_ref, wvt_ref, wft_ref, wglu_ref, bf_ref,
                           q_ref, ktb_ref, vtb_ref, kt_ref, vt_ref, lft_ref, glu_ref, *, d_conv):
    h = _rms(x_ref[...], g_ref[...]).astype(bf16)
    q_ref[...] = (_dot(h, wq_ref[...]) * (HEAD_DIM ** -0.5)).astype(bf16)
    kt = _dot_nt(wkt_ref[...], h)
    kt_ref[...] = kt
    ktb_ref[...] = kt.astype(bf16)
    vt = _dot_nt(wvt_ref[...], h)
    vt_ref[...] = vt
    vtb_ref[...] = vt.astype(bf16)
    lft_ref[...] = _log_sigmoid(_dot_nt(wft_ref[...], h) + bf_ref[...])
    a = _dot(h, wglu_ref[:, 0:d_conv])
    b = _dot(h, wglu_ref[:, d_conv:2 * d_conv])
    glu_ref[...] = a * jax.nn.sigmoid(b)


def _even_in_prompt(x, g, wq, wkt, wvt, wft, wglu, bf_col, *, tm, batch, seq):
    m, d = x.shape
    d_attn = wq.shape[1]
    n_heads = wft.shape[0]
    d_conv = wglu.shape[1] // 2
    tiles_per_seq = seq // tm
    kern = functools.partial(_even_in_prompt_kernel, d_conv=d_conv)
    t_spec = lambda rows: pl.BlockSpec((None, rows, tm), lambda i: (i // tiles_per_seq, 0, i % tiles_per_seq))
    out_shape = (
        jax.ShapeDtypeStruct((m, d_attn), bf16),
        jax.ShapeDtypeStruct((batch, d_attn, seq), bf16),
        jax.ShapeDtypeStruct((batch, d_attn, seq), bf16),
        jax.ShapeDtypeStruct((batch, d_attn, seq), f32),
        jax.ShapeDtypeStruct((batch, d_attn, seq), f32),
        jax.ShapeDtypeStruct((batch, n_heads, seq), f32),
        jax.ShapeDtypeStruct((m, d_conv), f32),
    )
    args = (x, g, wq, wkt, wvt, wft, wglu, bf_col)
    return pl.pallas_call(
        kern,
        grid=(m // tm,),
        in_specs=[_row_spec(tm, d)] + [_const_spec(a.shape) for a in args[1:]],
        out_specs=(_row_spec(tm, d_attn), t_spec(d_attn), t_spec(d_attn), t_spec(d_attn), t_spec(d_attn),
                   t_spec(n_heads), _row_spec(tm, d_conv)),
        out_shape=out_shape,
        compiler_params=_cparams(("parallel",)),
        name="even_in_proj",
    )(*args)


def _even_in_sample_kernel(x_ref, g_ref, wq_ref, wkt_ref, wvt_ref, wft_ref, wglu_ref, bf_ref,
                           q_ref, k_ref, v_ref, lf_ref, glu_ref, *, d_conv):
    h = _rms(x_ref[...], g_ref[...]).astype(bf16)
    q_ref[...] = (_dot(h, wq_ref[...]) * (HEAD_DIM ** -0.5)).astype(bf16)
    k_ref[...] = _dot_nt(h, wkt_ref[...])
    v_ref[...] = _dot_nt(h, wvt_ref[...])
    lf_ref[...] = _log_sigmoid(_dot_nt(h, wft_ref[...]) + bf_ref[...])
    a = _dot(h, wglu_ref[:, 0:d_conv])
    b = _dot(h, wglu_ref[:, d_conv:2 * d_conv])
    glu_ref[...] = a * jax.nn.sigmoid(b)


def _even_in_sample(x, g, wq, wkt, wvt, wft, wglu, bf_row):
    m, d = x.shape
    d_attn = wq.shape[1]
    n_heads = wft.shape[0]
    d_conv = wglu.shape[1] // 2
    kern = functools.partial(_even_in_sample_kernel, d_conv=d_conv)
    out_shape = (
        jax.ShapeDtypeStruct((m, d_attn), bf16),
        jax.ShapeDtypeStruct((m, d_attn), f32),
        jax.ShapeDtypeStruct((m, d_attn), f32),
        jax.ShapeDtypeStruct((m, n_heads), f32),
        jax.ShapeDtypeStruct((m, d_conv), f32),
    )
    args = (x, g, wq, wkt, wvt, wft, wglu, bf_row)
    return pl.pallas_call(
        kern,
        grid=(1,),
        in_specs=[_const_spec(a.shape) for a in args],
        out_specs=tuple(_const_spec(o.shape) for o in out_shape),
        out_shape=out_shape,
        compiler_params=_cparams(("arbitrary",)),
        name="even_in_proj_sample",
    )(*args)


def _cumsum_kernel(f_ref, c_ref):
    x = f_ref[...]
    n = x.shape[-1]
    lane = lax.broadcasted_iota(jnp.int32, x.shape, 1)
    d = 1
    while d < n:
        x = x + jnp.where(lane >= d, pltpu.roll(x, d, 1), 0.0)
        d *= 2
    c_ref[...] = x


def _cumsum_lanes(f_t):
    b, h, s = f_t.shape
    return pl.pallas_call(
        _cumsum_kernel,
        grid=(b,),
        in_specs=[pl.BlockSpec((None, h, s), lambda i: (i, 0, 0))],
        out_specs=pl.BlockSpec((None, h, s), lambda i: (i, 0, 0)),
        out_shape=jax.ShapeDtypeStruct((b, h, s), f32),
        compiler_params=_cparams(("parallel",)),
        name="logf_cumsum",
    )(f_t)


def _fox_prompt_kernel(q_ref, k_ref, v_ref, c_ref, o_ref, m_ref, acc_ref, *, tq, tk, n_heads):
    qi = pl.program_id(1)
    pair = 2 * HEAD_DIM
    lane = lax.broadcasted_iota(jnp.int32, (tq, pair), 1)
    first = lane < HEAD_DIM
    vrow = lax.broadcasted_iota(jnp.int32, (pair, tk), 0)
    sum_lane = (HEAD_DIM, 0)
    own_rows = (vrow < HEAD_DIM, vrow >= HEAD_DIM)
    keep = [jnp.where(own_rows[i], 1.0, 0.0).astype(bf16) for i in range(2)]
    ones_row = [jnp.where(vrow == sum_lane[i], 1.0, 0.0).astype(bf16) for i in range(2)]

    qm = []
    for h in range(n_heads):
        q = q_ref[:, (h // 2) * pair:(h // 2 + 1) * pair]
        zero = jnp.zeros_like(q)
        qm.append(jnp.where(first, q, zero) if h % 2 == 0 else jnp.where(first, zero, q))

    m_ref[...] = jnp.full(m_ref.shape, NEG, f32)
    acc_ref[...] = jnp.zeros(acc_ref.shape, f32)

    def block(j, masked):
        start = pl.multiple_of(j * tk, tk)
        for h in range(n_heads):
            rows = slice((h // 2) * pair, (h // 2 + 1) * pair)
            kb = k_ref[rows, pl.ds(start, tk)]
            vb = v_ref[rows, pl.ds(start, tk)]
            va = vb * keep[h % 2] + ones_row[h % 2]
            s = _dot(qm[h], kb) - c_ref[h:h + 1, pl.ds(start, tk)]
            if masked:
                row = lax.broadcasted_iota(jnp.int32, (tq, tk), 0)
                col = lax.broadcasted_iota(jnp.int32, (tq, tk), 1)
                s = jnp.where(col <= row, s, NEG)
            m_prev = m_ref[h]
            m_new = jnp.maximum(m_prev, jnp.max(s, axis=1, keepdims=True))
            p = jnp.exp(s - jnp.concatenate([m_new] * (tk // pair), axis=1))
            m_ref[h] = m_new
            acc_ref[h] = jnp.exp(m_prev - m_new) * acc_ref[h] + _dot_nt(p.astype(bf16), va)

    def body(j, carry):
        block(j, False)
        return carry

    n_full = qi * (tq // tk)
    lax.fori_loop(0, n_full, body, 0)
    block(n_full, True)
    for h in range(0, n_heads, 2):
        a0, a1 = acc_ref[h], acc_ref[h + 1]
        o0 = a0 / a0[:, sum_lane[0]:sum_lane[0] + 1]
        o1 = a1 / a1[:, sum_lane[1]:sum_lane[1] + 1]
        o_ref[:, (h // 2) * pair:(h // 2 + 1) * pair] = jnp.where(first, o0, o1).astype(o_ref.dtype)


def _fox_prompt(q, kt, vt, c, *, batch, seq):
    n, d_attn = q.shape
    n_heads = c.shape[1]
    tq, tk = ATTN_TQ, ATTN_TK
    assert tq == tk and seq % tq == 0
    assert 2 * HEAD_DIM == LANES and n_heads % 2 == 0, "one 128-wide block holds exactly two heads"
    nq = seq // tq
    kern = functools.partial(_fox_prompt_kernel, tq=tq, tk=tk, n_heads=n_heads)
    return pl.pallas_call(
        kern,
        grid=(batch, nq),
        in_specs=[
            pl.BlockSpec((tq, d_attn), lambda b, i: (b * nq + i, 0)),
            pl.BlockSpec((None, d_attn, seq), lambda b, i: (b, 0, 0)),
            pl.BlockSpec((None, d_attn, seq), lambda b, i: (b, 0, 0)),
            pl.BlockSpec((None, n_heads, seq), lambda b, i: (b, 0, 0)),
        ],
        out_specs=pl.BlockSpec((tq, d_attn), lambda b, i: (b * nq + i, 0)),
        out_shape=jax.ShapeDtypeStruct((n, d_attn), bf16),
        scratch_shapes=[pltpu.VMEM((n_heads, tq, LANES), f32), pltpu.VMEM((n_heads, tq, LANES), f32)],
        compiler_params=_cparams(("parallel", "arbitrary")),
        name="fox_prompt_attention",
    )(q, kt, vt, c)


def _conv_ln_silu(y, cb, lg, lb):
    return _silu(_layernorm(y + cb, lg, lb))


def _even_out_kernel(x_ref, attn_ref, glu_ref, halo_ref, cw_ref, cb_ref, lg_ref, lb_ref, wa_ref, wc_ref,
                     o_ref, ext_ref, sh_ref, conv_ref, *, tm, tiles_per_seq):
    i = pl.program_id(0)
    starts_sequence = (i % tiles_per_seq) == 0
    ext_ref[0:CONV_HALO, :] = jnp.where(starts_sequence, 0.0, halo_ref[...])
    ext_ref[CONV_HALO:CONV_HALO + tm, :] = glu_ref[...]
    d_conv = glu_ref.shape[1]
    off = CONV_HALO - (CONV_WIDTH - 1)
    sh_rows = sh_ref.shape[1]
    for b in range(1, SUBLANES):
        sh_ref[b - 1] = ext_ref[b:b + sh_rows, :]
    for c in range(d_conv // LANES):
        cs = slice(c * LANES, (c + 1) * LANES)
        for r in range(tm // CONV_ROWS):
            acc = jnp.zeros((CONV_ROWS, LANES), f32)
            for t in range(CONV_WIDTH):
                a, b = divmod(off + t, SUBLANES)
                lo = r * CONV_ROWS + a * SUBLANES
                src = ext_ref[lo:lo + CONV_ROWS, cs] if b == 0 else sh_ref[b - 1, lo:lo + CONV_ROWS, cs]
                acc = acc + cw_ref[t:t + 1, cs] * src
            conv_ref[r * CONV_ROWS:(r + 1) * CONV_ROWS, cs] = acc
    conv = _conv_ln_silu(conv_ref[...], cb_ref[...], lg_ref[...], lb_ref[...]).astype(bf16)
    o_ref[...] = x_ref[...] + _dot(attn_ref[...], wa_ref[...]) + _dot(conv, wc_ref[...])


def _even_out(x, attn, glu, cw, cb, lg, lb, wa, wc, *, tm, seq):
    m, d = x.shape
    d_attn = attn.shape[1]
    d_conv = glu.shape[1]
    assert seq % tm == 0 and tm % CONV_HALO == 0
    tiles_per_seq = seq // tm
    halo_per_tile = tm // CONV_HALO
    kern = functools.partial(_even_out_kernel, tm=tm, tiles_per_seq=tiles_per_seq)
    return pl.pallas_call(
        kern,
        grid=(m // tm,),
        in_specs=[
            _row_spec(tm, d), _row_spec(tm, d_attn), _row_spec(tm, d_conv),
            pl.BlockSpec((CONV_HALO, d_conv), lambda i: (jnp.maximum(i * halo_per_tile - 1, 0), 0)),
            _const_spec(cw.shape), _const_spec(cb.shape), _const_spec(lg.shape), _const_spec(lb.shape),
            _const_spec(wa.shape), _const_spec(wc.shape),
        ],
        out_specs=_row_spec(tm, d),
        out_shape=jax.ShapeDtypeStruct((m, d), f32),
        scratch_shapes=[pltpu.VMEM((CONV_HALO + tm, d_conv), f32),
                        pltpu.VMEM((SUBLANES - 1, CONV_HALO + tm - SUBLANES, d_conv), f32),
                        pltpu.VMEM((tm, d_conv), f32)],
        compiler_params=_cparams(("parallel",)),
        name="even_conv_out_proj",
    )(x, attn, glu, glu, cw, cb, lg, lb, wa, wc)


def _even_out_sample_kernel(x_ref, attn_ref, glu_ref, st_ref, cw_ref, cb_ref, lg_ref, lb_ref, wa_ref, wc_ref,
                            o_ref):
    y = cw_ref[CONV_WIDTH - 1:CONV_WIDTH, :] * glu_ref[...]
    for t in range(CONV_WIDTH - 1):
        y = y + cw_ref[t:t + 1, :] * st_ref[t]
    conv = _conv_ln_silu(y, cb_ref[...], lg_ref[...], lb_ref[...]).astype(bf16)
    o_ref[...] = (x_ref[...] + _dot(attn_ref[...].astype(bf16), wa_ref[...]) + _dot(conv, wc_ref[...]))


def _even_out_sample(x, attn, glu, state_t, cw, cb, lg, lb, wa, wc):
    m, d = x.shape
    args = (x, attn, glu, state_t, cw, cb, lg, lb, wa, wc)
    return pl.pallas_call(
        _even_out_sample_kernel,
        grid=(1,),
        in_specs=[_const_spec(a.shape) for a in args],
        out_specs=_const_spec((m, d)),
        out_shape=jax.ShapeDtypeStruct((m, d), f32),
        compiler_params=_cparams(("arbitrary",)),
        name="even_conv_out_proj_sample",
    )(*args)


def _suffix_sums(f):
    lane = lax.broadcasted_iota(jnp.int32, f.shape, 1)
    x = f
    d = 1
    while d < PAGE_SIZE:
        x = x + jnp.where(lane + d < PAGE_SIZE, pltpu.roll(x, PAGE_SIZE - d, 1), 0.0)
        d *= 2
    return x - f


def _fox_sample_scores_kernel(pt_ref, q_ref, kn_ref, lfn_ref, ck_ref, cf_ref,
                              e_ref, w_ref, l_ref, fl_ref, kbuf, fbuf, sems, s_ref, carry_ref, pmax_ref,
                              *, layer, n_heads, pages):
    db, n_pages = pt_ref.shape
    n_groups = n_pages // pages
    n_steps = db * n_groups
    d_attn = q_ref.shape[-1]
    head_of_lane = lax.broadcasted_iota(jnp.int32, (n_heads, d_attn), 1) // HEAD_DIM
    own = head_of_lane == lax.broadcasted_iota(jnp.int32, (n_heads, d_attn), 0)

    def page_copies(t, slot, p):
        b = lax.div(t, n_groups)
        j = t - b * n_groups
        page = pt_ref[b, n_pages - 1 - (j * pages + p)]
        return (pltpu.make_async_copy(ck_ref.at[layer, page], kbuf.at[slot, p], sems.at[0, slot]),
                pltpu.make_async_copy(cf_ref.at[layer, page], fbuf.at[slot, p], sems.at[1, slot]))

    def start_group(t, slot):
        def one(p, c):
            for cp in page_copies(t, slot, p):
                cp.start()
            return c
        lax.fori_loop(0, pages, one, 0)

    def wait_group(t, slot):
        def one(p, c):
            for cp in page_copies(t, slot, p):
                cp.wait()
            return c
        lax.fori_loop(0, pages, one, 0)

    start_group(0, 0)

    def step(t, carry_unused):
        slot = lax.rem(t, 2)
        b = lax.div(t, n_groups)
        j = t - b * n_groups

        @pl.when(t + 1 < n_steps)
        def _():
            start_group(t + 1, 1 - slot)

        q = q_ref[b].astype(f32)
        qbd32 = jnp.where(own, jnp.broadcast_to(q, (n_heads, d_attn)), 0.0)
        qbd = qbd32.astype(bf16)

        @pl.when(j == 0)
        def _():
            carry_ref[...] = jnp.zeros(carry_ref.shape, f32)
            pmax_ref[...] = jnp.full(pmax_ref.shape, NEG, f32)

        wait_group(t, slot)

        f_all = fbuf[slot].reshape(pages * n_heads, PAGE_SIZE)
        suffix = _suffix_sums(f_all)
        totals = jnp.sum(f_all, axis=1, keepdims=True)
        carry = carry_ref[...]
        pmax = pmax_ref[...]
        page_lane = lax.broadcasted_iota(jnp.int32, pmax.shape, 1)
        for p in range(pages):
            rows = slice(p * n_heads, (p + 1) * n_heads)
            s = _dot(qbd, kbuf[slot, p].astype(bf16)) + (carry + suffix[rows])
            logical = n_pages - 1 - (j * pages + p)
            s_ref[:, pl.ds(pl.multiple_of(logical * PAGE_SIZE, PAGE_SIZE), PAGE_SIZE)] = s
            pmax = jnp.where(page_lane == logical, jnp.max(s, axis=1, keepdims=True), pmax)
            carry = carry + totals[rows]
        carry_ref[...] = carry
        pmax_ref[...] = pmax

        @pl.when(j == n_groups - 1)
        def _():
            s_new = jnp.sum(qbd32 * kn_ref[b], axis=1, keepdims=True) - lfn_ref[b]
            m = jnp.maximum(jnp.max(pmax, axis=1, keepdims=True), s_new)
            e = jnp.exp(s_ref[...] - m)
            w = jnp.exp(s_new - m)
            e_ref[b] = e
            w_ref[b] = w
            l_ref[b] = jnp.sum(e, axis=1, keepdims=True) + w
            fl_ref[b] = (jnp.max(jnp.exp(pmax - m), axis=0, keepdims=True) > 0.0).astype(jnp.int32)

        return carry_unused

    lax.fori_loop(0, n_steps, step, 0)


def _fox_sample_pv_kernel(pt_ref, fl_ref, e_ref, w_ref, l_ref, vn_ref, cv_ref, o_ref, vbuf, sems, acc_ref,
                          seq_ref, page_ref, *, layer, n_heads, pages):
    db, n_pages = pt_ref.shape
    d_attn = vn_ref.shape[-1]
    head_of_lane = lax.broadcasted_iota(jnp.int32, (1, n_heads, d_attn), 2) // HEAD_DIM
    own = head_of_lane == lax.broadcasted_iota(jnp.int32, (1, n_heads, d_attn), 1)

    def scan_seq(b, c):
        def scan_page(lp, c):
            seq_ref[c] = b
            page_ref[c] = lp
            return c + (fl_ref[b, lp] != 0).astype(jnp.int32)
        return lax.fori_loop(0, n_pages, scan_page, c)
    n_entries = lax.fori_loop(0, db, scan_seq, 0)
    n_groups = lax.div(n_entries + (pages - 1), pages)

    def entry(g, k):
        i = jnp.minimum(g * pages + k, n_entries - 1)
        return seq_ref[i], page_ref[i]

    def page_copies(g, slot):
        copies = []
        for k in range(pages):
            b, lp = entry(g, k)
            copies.append(pltpu.make_async_copy(cv_ref.at[layer, pt_ref[b, lp]], vbuf.at[slot, k], sems.at[slot]))
        return copies

    acc_ref[...] = w_ref[...] * vn_ref[...]

    @pl.when(n_groups > 0)
    def _():
        for cp in page_copies(0, 0):
            cp.start()

    def group(g, carry_unused):
        slot = lax.rem(g, 2)

        @pl.when(g + 1 < n_groups)
        def _():
            for cp in page_copies(g + 1, 1 - slot):
                cp.start()

        for cp in page_copies(g, slot):
            cp.wait()
        seqs, partial = [], []
        for k in range(pages):
            b, lp = entry(g, k)
            e = e_ref[b, :, pl.ds(pl.multiple_of(lp * PAGE_SIZE, PAGE_SIZE), PAGE_SIZE)]
            e = jnp.where(g * pages + k < n_entries, e, 0.0)
            seqs.append(b)
            partial.append(_dot_nt(e.astype(bf16), vbuf[slot, k].astype(bf16)))
        for b, r in zip(seqs, partial):
            acc_ref[b] += r
        return carry_unused

    lax.fori_loop(0, n_groups, group, 0)
    o_ref[...] = jnp.sum(jnp.where(own, acc_ref[...] / l_ref[...], 0.0), axis=1, keepdims=True)


def _fox_sample(page_table, q, k_new, v_new, lf_new, cache_k, cache_v, cache_f_t, layer):
    db, n_pages = page_table.shape
    n_heads = lf_new.shape[1]
    d_attn = q.shape[1]
    n_past = n_pages * PAGE_SIZE
    pages = PAGES_PER_STEP
    assert n_pages % pages == 0
    hbm = pl.BlockSpec(memory_space=pl.ANY)
    q3, kn3, vn3 = q.reshape(db, 1, d_attn), k_new.reshape(db, 1, d_attn), v_new.reshape(db, 1, d_attn)
    lfn3 = lf_new.reshape(db, n_heads, 1)

    whole1 = lambda shape: pl.BlockSpec(shape, lambda i, pt: (0,) * len(shape))
    e, w, l, flags = pl.pallas_call(
        functools.partial(_fox_sample_scores_kernel, layer=layer, n_heads=n_heads, pages=pages),
        grid_spec=pltpu.PrefetchScalarGridSpec(
            num_scalar_prefetch=1,
            grid=(1,),
            in_specs=[whole1(q3.shape), whole1(kn3.shape), whole1(lfn3.shape), hbm, hbm],
            out_specs=(whole1((db, n_heads, n_past)), whole1((db, n_heads, 1)), whole1((db, n_heads, 1)),
                       whole1((db, 1, n_pages))),
            scratch_shapes=[
                pltpu.VMEM((2, pages, d_attn, PAGE_SIZE), f32),
                pltpu.VMEM((2, pages, n_heads, PAGE_SIZE), f32),
                pltpu.SemaphoreType.DMA((2, 2)),
                pltpu.VMEM((n_heads, n_past), f32),
                pltpu.VMEM((n_heads, 1), f32),
                pltpu.VMEM((n_heads, n_pages), f32)],
        ),
        out_shape=(jax.ShapeDtypeStruct((db, n_heads, n_past), f32), jax.ShapeDtypeStruct((db, n_heads, 1), f32),
                   jax.ShapeDtypeStruct((db, n_heads, 1), f32), jax.ShapeDtypeStruct((db, 1, n_pages), jnp.int32)),
        compiler_params=_cparams(("arbitrary",)),
        name="fox_sample_scores",
    )(page_table, q3, kn3, lfn3, cache_k, cache_f_t)

    whole2 = lambda shape: pl.BlockSpec(shape, lambda i, pt, fl: (0,) * len(shape))
    pv_pages = PV_PAGES_PER_STEP
    out = pl.pallas_call(
        functools.partial(_fox_sample_pv_kernel, layer=layer, n_heads=n_heads, pages=pv_pages),
        grid_spec=pltpu.PrefetchScalarGridSpec(
            num_scalar_prefetch=2,
            grid=(1,),
            in_specs=[whole2(e.shape), whole2(w.shape), whole2(l.shape), whole2(vn3.shape), hbm],
            out_specs=whole2((db, 1, d_attn)),
            scratch_shapes=[
                pltpu.VMEM((2, pv_pages, d_attn, PAGE_SIZE), f32),
                pltpu.SemaphoreType.DMA((2,)),
                pltpu.VMEM((db, n_heads, d_attn), f32),
                pltpu.SMEM((db * n_pages,), jnp.int32),
                pltpu.SMEM((db * n_pages,), jnp.int32)],
        ),
        out_shape=jax.ShapeDtypeStruct((db, 1, d_attn), f32),
        compiler_params=_cparams(("arbitrary",)),
        name="fox_sample_pv",
    )(page_table, flags.reshape(db, n_pages), e, w, l, vn3, cache_v)
    return out.reshape(db, d_attn)


def _sgu_kernel(x_ref, g_ref, win_ref, lg_ref, lb_ref, ws_ref, bs_ref, wout_ref, *out_refs, tm, sample):
    d_sgu = wout_ref.shape[0]
    x = x_ref[...]
    h = _rms(x, g_ref[...]).astype(bf16)
    u = _gelu(_dot(h, win_ref[:, 0:d_sgu]))
    v = _gelu(_dot(h, win_ref[:, d_sgu:2 * d_sgu]))
    vn = _layernorm(v, lg_ref[...], lb_ref[...])
    if sample:
        o_ref, vn_ref = out_refs
        vn_ref[...] = vn
        gated = u * (vn * ws_ref[...] + bs_ref[...])
    else:
        (o_ref,) = out_refs
        vnb = vn.astype(bf16)
        group = d_sgu // N_SGU_GROUPS
        rows = []
        for r in range(tm // CHUNK):
            rs = slice(r * CHUNK, (r + 1) * CHUNK)
            cols = [_dot(ws_ref[g], vnb[rs, g * group:(g + 1) * group]) for g in range(N_SGU_GROUPS)]
            rows.append(jnp.concatenate(cols, axis=1) + bs_ref[...])
        mixed = jnp.concatenate(rows, axis=0)
        gated = u * mixed
    o_ref[...] = x + _dot(gated.astype(bf16), wout_ref[...])


def _sgu(x, g, win, lg, lb, ws, bs, wout, *, tm, sample):
    m, d = x.shape
    d_sgu = wout.shape[0]
    kern = functools.partial(_sgu_kernel, tm=tm, sample=sample)
    out_shape = [jax.ShapeDtypeStruct((m, d), f32)]
    out_specs = [_row_spec(tm, d)]
    if sample:
        out_shape.append(jax.ShapeDtypeStruct((m, d_sgu), f32))
        out_specs.append(_row_spec(tm, d_sgu))
    outs = pl.pallas_call(
        kern,
        grid=(m // tm,),
        in_specs=[_row_spec(tm, d), _const_spec(g.shape), _const_spec(win.shape), _const_spec(lg.shape),
                  _const_spec(lb.shape), _const_spec(ws.shape), _const_spec(bs.shape), _const_spec(wout.shape)],
        out_specs=tuple(out_specs),
        out_shape=tuple(out_shape),
        compiler_params=_cparams(("parallel",)),
        name="sgu_mixer_sample" if sample else "sgu_mixer",
    )(x, g, win, lg, lb, ws, bs, wout)
    return outs if sample else outs[0]


def _ffn_kernel(x_ref, g_ref, wg_ref, wu_ref, wd_ref, gf_ref, o_ref, *, tf, final):
    x = x_ref[...]
    h = _rms(x, g_ref[...]).astype(bf16)
    acc = x
    d_ff = wg_ref.shape[1]
    for c in range(d_ff // tf):
        sl = slice(c * tf, (c + 1) * tf)
        a = _silu(_dot(h, wg_ref[:, sl])) * _dot(h, wu_ref[:, sl])
        acc = acc + _dot(a.astype(bf16), wd_ref[sl, :])
    if final:
        acc = _rms(acc, gf_ref[...])
    o_ref[...] = acc


def _ffn(x, g, wg, wu, wd, gf, *, tm, final):
    m, d = x.shape
    d_ff = wg.shape[1]
    tf = d_ff // 2 if (d_ff // 2) % LANES == 0 else d_ff
    kern = functools.partial(_ffn_kernel, tf=tf, final=final)
    return pl.pallas_call(
        kern,
        grid=(m // tm,),
        in_specs=[_row_spec(tm, d), _const_spec(g.shape), _const_spec(wg.shape), _const_spec(wu.shape),
                  _const_spec(wd.shape), _const_spec(gf.shape)],
        out_specs=_row_spec(tm, d),
        out_shape=jax.ShapeDtypeStruct((m, d), f32),
        compiler_params=_cparams(("parallel",)),
        name="swiglu_ffn_final" if final else "swiglu_ffn",
    )(x, g, wg, wu, wd, gf)


def kernel(x_prompt, x_sample, cache_k, cache_v, cache_logf, page_table, state_conv, norm_mix, norm_ffn, norm_final, w_in_even, b_forget, conv_w, conv_b, conv_ln_g, conv_ln_b, w_out_even, w_in_odd, sgu_ln_g, sgu_ln_b, sgu_w, sgu_b, w_out_odd, w_gate, w_up, w_down):
    batch, seq, d_model = x_prompt.shape
    db, dec_seq, _ = x_sample.shape
    assert dec_seq == 1, "the sample path handles one new token per sequence"
    depth = norm_mix.shape[0]
    n_heads = b_forget.shape[1]
    d_attn = n_heads * HEAD_DIM
    d_conv = conv_w.shape[2]
    n_phys = cache_k.shape[1]
    n_tok = batch * seq
    row = lambda a: a.reshape(1, -1)

    xp = x_prompt.reshape(n_tok, d_model)
    xs = x_sample.reshape(db, d_model)
    tm_p = ROW_TILE
    tm_s = db

    feature_major = lambda a: jnp.transpose(a, (0, 1, 3, 4, 2)).reshape(a.shape[0], n_phys, d_attn, PAGE_SIZE)
    cache_kt = feature_major(cache_k)
    cache_vt = feature_major(cache_v)
    cache_f_t = jnp.swapaxes(cache_logf.astype(f32), 2, 3)

    kp_l, vp_l, fp_l, cp_l = [], [], [], []
    ks_l, vs_l, fs_l, cs_l = [], [], [], []
    sgu_l = []
    for l in range(depth):
        g_mix = row(norm_mix[l])
        last = l == depth - 1
        if l % 2 == 0:
            i = l // 2
            w_in = w_in_even[i]
            w_in_t = w_in.T
            wq = w_in[:, 0:d_attn].astype(bf16)
            wkt = w_in_t[d_attn:2 * d_attn].astype(bf16)
            wvt = w_in_t[2 * d_attn:3 * d_attn].astype(bf16)
            o = 3 * d_attn
            wft = w_in_t[o:o + n_heads].astype(bf16)
            wglu = w_in[:, o + n_heads:].astype(bf16)
            wa = w_out_even[i][0:d_attn].astype(bf16)
            wc = w_out_even[i][d_attn:].astype(bf16)
            cw, cb, lg, lb = conv_w[i], row(conv_b[i]), row(conv_ln_g[i]), row(conv_ln_b[i])

            qb, ktb, vtb, kt, vt, lf_t, glu = _even_in_prompt(
                xp, g_mix, wq, wkt, wvt, wft, wglu, b_forget[i].reshape(n_heads, 1), tm=tm_p, batch=batch, seq=seq)
            c = _cumsum_lanes(lf_t)
            attn = _fox_prompt(qb, ktb, vtb, c, batch=batch, seq=seq)
            xp = _even_out(xp, attn, glu, cw, cb, lg, lb, wa, wc, tm=tm_p, seq=seq)
            token_major = lambda a: jnp.transpose(a.reshape(batch, n_heads, HEAD_DIM, seq), (0, 3, 1, 2))
            kp_l.append(token_major(kt))
            vp_l.append(token_major(vt))
            fp_l.append(jnp.swapaxes(lf_t, 1, 2))
            cp_l.append(glu.reshape(batch, seq, d_conv)[:, seq - (CONV_WIDTH - 1):])

            qb2, k2, v2, lf2, glu2 = _even_in_sample(xs, g_mix, wq, wkt, wvt, wft, wglu, row(b_forget[i]))
            attn2 = _fox_sample(page_table, qb2, k2, v2, lf2, cache_kt, cache_vt, cache_f_t, i)
            state_t = jnp.swapaxes(state_conv[i], 0, 1)
            xs = _even_out_sample(xs, attn2, glu2, state_t, cw, cb, lg, lb, wa, wc)
            ks_l.append(k2.reshape(db, 1, n_heads, HEAD_DIM))
            vs_l.append(v2.reshape(db, 1, n_heads, HEAD_DIM))
            fs_l.append(lf2.reshape(db, 1, n_heads))
            cs_l.append(jnp.concatenate([state_conv[i][:, 1:], glu2[:, None, :]], axis=1))
        else:
            j = l // 2
            d_sgu = w_out_odd.shape[1]
            group = d_sgu // N_SGU_GROUPS
            win = w_in_odd[j].astype(bf16)
            wout = w_out_odd[j].astype(bf16)
            lg, lb = row(sgu_ln_g[j]), row(sgu_ln_b[j])
            causal = jnp.tril(jnp.ones((CHUNK, CHUNK), dtype=bool))
            ws = jnp.where(causal, sgu_w[j], 0).astype(bf16)
            bs = jnp.repeat(jnp.swapaxes(sgu_b[j], 0, 1), group, axis=1)
            xp = _sgu(xp, g_mix, win, lg, lb, ws, bs, wout, tm=tm_p, sample=False)
            ws0 = row(jnp.repeat(sgu_w[j][:, 0, 0], group))
            bs0 = row(jnp.repeat(sgu_b[j][:, 0], group))
            xs, vn_s = _sgu(xs, g_mix, win, lg, lb, ws0, bs0, wout, tm=tm_s, sample=True)
            sgu_l.append(vn_s.reshape(db, 1, d_sgu))
        g_ffn = row(norm_ffn[l])
        gf = row(norm_final)
        wg, wu, wd = w_gate[l].astype(bf16), w_up[l].astype(bf16), w_down[l].astype(bf16)
        xp = _ffn(xp, g_ffn, wg, wu, wd, gf, tm=FFN_ROW_TILE, final=last)
        xs = _ffn(xs, g_ffn, wg, wu, wd, gf, tm=tm_s, final=last)

    return (xp.reshape(batch, seq, d_model), xs.reshape(db, 1, d_model),
            jnp.stack(kp_l), jnp.stack(vp_l), jnp.stack(fp_l), jnp.stack(cp_l),
            jnp.stack(ks_l), jnp.stack(vs_l), jnp.stack(fs_l), jnp.stack(cs_l),
            jnp.stack(sgu_l))
```

```python
import functools

import jax
import jax.numpy as jnp
from jax import lax
from jax.experimental import pallas as pl
from jax.experimental.pallas import tpu as pltpu

HEAD_DIM = 64
CONV_WIDTH = 31
CHUNK = 128
N_SGU_GROUPS = 8
PAGE_SIZE = 128
RMS_EPS = 1e-6
LN_EPS = 1e-5

LANES = 128
SUBLANES = 8
VMEM_LIMIT = 56 * 1024 * 1024

ROW_TILE = 512
FFN_ROW_TILE = 1024
ATTN_TQ = 512
ATTN_TK = 512
CONV_HALO = 32
CONV_ROWS = 64
PAGES_PER_STEP = 32
PV_PAGES_PER_STEP = 8
NEG = -1e30

bf16 = jnp.bfloat16
f32 = jnp.float32


def _cparams(sem):
    return pltpu.CompilerParams(dimension_semantics=sem, vmem_limit_bytes=VMEM_LIMIT)


def _dot(a, b):
    return jnp.dot(a, b, preferred_element_type=f32)


def _dot_nt(a, b):
    return lax.dot_general(a, b, (((1,), (1,)), ((), ())), preferred_element_type=f32)


def _rms(x, g):
    ms = jnp.mean(x * x, axis=-1, keepdims=True)
    return x * lax.rsqrt(ms + RMS_EPS) * g


def _layernorm(x, g, b):
    mu = jnp.mean(x, axis=-1, keepdims=True)
    xc = x - mu
    var = jnp.mean(xc * xc, axis=-1, keepdims=True)
    return xc * lax.rsqrt(var + LN_EPS) * g + b


def _silu(x):
    return x * jax.nn.sigmoid(x)


def _gelu(x):
    return 0.5 * x * (1.0 + lax.erf(x * (2.0 ** -0.5)))


def _log_sigmoid(z):
    return jnp.minimum(z, 0.0) - jnp.log1p(jnp.exp(-jnp.abs(z)))


def _const_spec(shape):
    nd = len(shape)
    return pl.BlockSpec(shape, lambda *_: (0,) * nd, pipeline_mode=pl.Buffered(1))


def _row_spec(tm, width):
    return pl.BlockSpec((tm, width), lambda i: (i, 0))


def _even_in_prompt_kernel(x_ref, g_ref, wq_ref, wkt_ref, wvt_ref, wft_ref, wglu_ref, bf_ref,
                           q_ref, ktb_ref, vtb_ref, kt_ref, vt_ref, lft_ref, glu_ref, *, d_conv):
    h = _rms(x_ref[...], g_ref[...]).astype(bf16)
    q_ref[...] = (_dot(h, wq_ref[...]) * (HEAD_DIM ** -0.5)).astype(bf16)
    kt = _dot_nt(wkt_ref[...], h)
    kt_ref[...] = kt
    ktb_ref[...] = kt.astype(bf16)
    vt = _dot_nt(wvt_ref[...], h)
    vt_ref[...] = vt
    vtb_ref[...] = vt.astype(bf16)
    lft_ref[...] = _log_sigmoid(_dot_nt(wft_ref[...], h) + bf_ref[...])
    a = _dot(h, wglu_ref[:, 0:d_conv])
    b = _dot(h, wglu_ref[:, d_conv:2 * d_conv])
    glu_ref[...] = a * jax.nn.sigmoid(b)


def _even_in_prompt(x, g, wq, wkt, wvt, wft, wglu, bf_col, *, tm, batch, seq):
    m, d = x.shape
    d_attn = wq.shape[1]
    n_heads = wft.shape[0]
    d_conv = wglu.shape[1] // 2
    tiles_per_seq = seq // tm
    kern = functools.partial(_even_in_prompt_kernel, d_conv=d_conv)
    t_spec = lambda rows: pl.BlockSpec((None, rows, tm), lambda i: (i // tiles_per_seq, 0, i % tiles_per_seq))
    out_shape = (
        jax.ShapeDtypeStruct((m, d_attn), bf16),
        jax.ShapeDtypeStruct((batch, d_attn, seq), bf16),
        jax.ShapeDtypeStruct((batch, d_attn, seq), bf16),
        jax.ShapeDtypeStruct((batch, d_attn, seq), f32),
        jax.ShapeDtypeStruct((batch, d_attn, seq), f32),
        jax.ShapeDtypeStruct((batch, n_heads, seq), f32),
        jax.ShapeDtypeStruct((m, d_conv), f32),
    )
    args = (x, g, wq, wkt, wvt, wft, wglu, bf_col)
    return pl.pallas_call(
        kern,
        grid=(m // tm,),
        in_specs=[_row_spec(tm, d)] + [_const_spec(a.shape) for a in args[1:]],
        out_specs=(_row_spec(tm, d_attn), t_spec(d_attn), t_spec(d_attn), t_spec(d_attn), t_spec(d_attn),
                   t_spec(n_heads), _row_spec(tm, d_conv)),
        out_shape=out_shape,
        compiler_params=_cparams(("parallel",)),
        name="even_in_proj",
    )(*args)


def _even_in_sample_kernel(x_ref, g_ref, wq_ref, wkt_ref, wvt_ref, wft_ref, wglu_ref, bf_ref,
                           q_ref, k_ref, v_ref, lf_ref, glu_ref, *, d_conv):
    h = _rms(x_ref[...], g_ref[...]).astype(bf16)
    q_ref[...] = (_dot(h, wq_ref[...]) * (HEAD_DIM ** -0.5)).astype(bf16)
    k_ref[...] = _dot_nt(h, wkt_ref[...])
    v_ref[...] = _dot_nt(h, wvt_ref[...])
    lf_ref[...] = _log_sigmoid(_dot_nt(h, wft_ref[...]) + bf_ref[...])
    a = _dot(h, wglu_ref[:, 0:d_conv])
    b = _dot(h, wglu_ref[:, d_conv:2 * d_conv])
    glu_ref[...] = a * jax.nn.sigmoid(b)


def _even_in_sample(x, g, wq, wkt, wvt, wft, wglu, bf_row):
    m, d = x.shape
    d_attn = wq.shape[1]
    n_heads = wft.shape[0]
    d_conv = wglu.shape[1] // 2
    kern = functools.partial(_even_in_sample_kernel, d_conv=d_conv)
    out_shape = (
        jax.ShapeDtypeStruct((m, d_attn), bf16),
        jax.ShapeDtypeStruct((m, d_attn), f32),
        jax.ShapeDtypeStruct((m, d_attn), f32),
        jax.ShapeDtypeStruct((m, n_heads), f32),
        jax.ShapeDtypeStruct((m, d_conv), f32),
    )
    args = (x, g, wq, wkt, wvt, wft, wglu, bf_row)
    return pl.pallas_call(
        kern,
        grid=(1,),
        in_specs=[_const_spec(a.shape) for a in args],
        out_specs=tuple(_const_spec(o.shape) for o in out_shape),
        out_shape=out_shape,
        compiler_params=_cparams(("arbitrary",)),
        name="even_in_proj_sample",
    )(*args)


def _cumsum_kernel(f_ref, c_ref):
    x = f_ref[...]
    n = x.shape[-1]
    lane = lax.broadcasted_iota(jnp.int32, x.shape, 1)
    d = 1
    while d < n:
        x = x + jnp.where(lane >= d, pltpu.roll(x, d, 1), 0.0)
        d *= 2
    c_ref[...] = x


def _cumsum_lanes(f_t):
    b, h, s = f_t.shape
    return pl.pallas_call(
        _cumsum_kernel,
        grid=(b,),
        in_specs=[pl.BlockSpec((None, h, s), lambda i: (i, 0, 0))],
        out_specs=pl.BlockSpec((None, h, s), lambda i: (i, 0, 0)),
        out_shape=jax.ShapeDtypeStruct((b, h, s), f32),
        compiler_params=_cparams(("parallel",)),
        name="logf_cumsum",
    )(f_t)


def _fox_prompt_kernel(q_ref, k_ref, v_ref, c_ref, o_ref, m_ref, acc_ref, *, tq, tk, n_heads):
    qi = pl.program_id(1)
    pair = 2 * HEAD_DIM
    lane = lax.broadcasted_iota(jnp.int32, (tq, pair), 1)
    first = lane < HEAD_DIM
    vrow = lax.broadcasted_iota(jnp.int32, (pair, tk), 0)
    sum_lane = (HEAD_DIM, 0)
    own_rows = (vrow < HEAD_DIM, vrow >= HEAD_DIM)
    keep = [jnp.where(own_rows[i], 1.0, 0.0).astype(bf16) for i in range(2)]
    ones_row = [jnp.where(vrow == sum_lane[i], 1.0, 0.0).astype(bf16) for i in range(2)]

    qm = []
    for h in range(n_heads):
        q = q_ref[:, (h // 2) * pair:(h // 2 + 1) * pair]
        zero = jnp.zeros_like(q)
        qm.append(jnp.where(first, q, zero) if h % 2 == 0 else jnp.where(first, zero, q))

    m_ref[...] = jnp.full(m_ref.shape, NEG, f32)
    acc_ref[...] = jnp.zeros(acc_ref.shape, f32)

    def block(j, masked):
        start = pl.multiple_of(j * tk, tk)
        for h in range(n_heads):
            rows = slice((h // 2) * pair, (h // 2 + 1) * pair)
            kb = k_ref[rows, pl.ds(start, tk)]
            vb = v_ref[rows, pl.ds(start, tk)]
            va = vb * keep[h % 2] + ones_row[h % 2]
            s = _dot(qm[h], kb) - c_ref[h:h + 1, pl.ds(start, tk)]
            if masked:
                row = lax.broadcasted_iota(jnp.int32, (tq, tk), 0)
                col = lax.broadcasted_iota(jnp.int32, (tq, tk), 1)
                s = jnp.where(col <= row, s, NEG)
            m_prev = m_ref[h]
            m_new = jnp.maximum(m_prev, jnp.max(s, axis=1, keepdims=True))
            p = jnp.exp(s - jnp.concatenate([m_new] * (tk // pair), axis=1))
            m_ref[h] = m_new
            acc_ref[h] = jnp.exp(m_prev - m_new) * acc_ref[h] + _dot_nt(p.astype(bf16), va)

    def body(j, carry):
        block(j, False)
        return carry

    n_full = qi * (tq // tk)
    lax.fori_loop(0, n_full, body, 0)
    block(n_full, True)
    for h in range(0, n_heads, 2):
        a0, a1 = acc_ref[h], acc_ref[h + 1]
        o0 = a0 / a0[:, sum_lane[0]:sum_lane[0] + 1]
        o1 = a1 / a1[:, sum_lane[1]:sum_lane[1] + 1]
        o_ref[:, (h // 2) * pair:(h // 2 + 1) * pair] = jnp.where(first, o0, o1).astype(o_ref.dtype)


def _fox_prompt(q, kt, vt, c, *, batch, seq):
    n, d_attn = q.shape
    n_heads = c.shape[1]
    tq, tk = ATTN_TQ, ATTN_TK
    assert tq == tk and seq % tq == 0
    assert 2 * HEAD_DIM == LANES and n_heads % 2 == 0, "one 128-wide block holds exactly two heads"
    nq = seq // tq
    kern = functools.partial(_fox_prompt_kernel, tq=tq, tk=tk, n_heads=n_heads)
    return pl.pallas_call(
        kern,
        grid=(batch, nq),
        in_specs=[
            pl.BlockSpec((tq, d_attn), lambda b, i: (b * nq + i, 0)),
            pl.BlockSpec((None, d_attn, seq), lambda b, i: (b, 0, 0)),
            pl.BlockSpec((None, d_attn, seq), lambda b, i: (b, 0, 0)),
            pl.BlockSpec((None, n_heads, seq), lambda b, i: (b, 0, 0)),
        ],
        out_specs=pl.BlockSpec((tq, d_attn), lambda b, i: (b * nq + i, 0)),
        out_shape=jax.ShapeDtypeStruct((n, d_attn), bf16),
        scratch_shapes=[pltpu.VMEM((n_heads, tq, LANES), f32), pltpu.VMEM((n_heads, tq, LANES), f32)],
        compiler_params=_cparams(("parallel", "arbitrary")),
        name="fox_prompt_attention",
    )(q, kt, vt, c)


def _conv_ln_silu(y, cb, lg, lb):
    return _silu(_layernorm(y + cb, lg, lb))


def _even_out_kernel(x_ref, attn_ref, glu_ref, halo_ref, cw_ref, cb_ref, lg_ref, lb_ref, wa_ref, wc_ref,
                     o_ref, ext_ref, sh_ref, conv_ref, *, tm, tiles_per_seq):
    i = pl.program_id(0)
    starts_sequence = (i % tiles_per_seq) == 0
    ext_ref[0:CONV_HALO, :] = jnp.where(starts_sequence, 0.0, halo_ref[...])
    ext_ref[CONV_HALO:CONV_HALO + tm, :] = glu_ref[...]
    d_conv = glu_ref.shape[1]
    off = CONV_HALO - (CONV_WIDTH - 1)
    sh_rows = sh_ref.shape[1]
    for b in range(1, SUBLANES):
        sh_ref[b - 1] = ext_ref[b:b + sh_rows, :]
    for c in range(d_conv // LANES):
        cs = slice(c * LANES, (c + 1) * LANES)
        for r in range(tm // CONV_ROWS):
            acc = jnp.zeros((CONV_ROWS, LANES), f32)
            for t in range(CONV_WIDTH):
                a, b = divmod(off + t, SUBLANES)
                lo = r * CONV_ROWS + a * SUBLANES
                src = ext_ref[lo:lo + CONV_ROWS, cs] if b == 0 else sh_ref[b - 1, lo:lo + CONV_ROWS, cs]
                acc = acc + cw_ref[t:t + 1, cs] * src
            conv_ref[r * CONV_ROWS:(r + 1) * CONV_ROWS, cs] = acc
    conv = _conv_ln_silu(conv_ref[...], cb_ref[...], lg_ref[...], lb_ref[...]).astype(bf16)
    o_ref[...] = x_ref[...] + _dot(attn_ref[...], wa_ref[...]) + _dot(conv, wc_ref[...])


def _even_out(x, attn, glu, cw, cb, lg, lb, wa, wc, *, tm, seq):
    m, d = x.shape
    d_attn = attn.shape[1]
    d_conv = glu.shape[1]
    assert seq % tm == 0 and tm % CONV_HALO == 0
    tiles_per_seq = seq // tm
    halo_per_tile = tm // CONV_HALO
    kern = functools.partial(_even_out_kernel, tm=tm, tiles_per_seq=tiles_per_seq)
    return pl.pallas_call(
        kern,
        grid=(m // tm,),
        in_specs=[
            _row_spec(tm, d), _row_spec(tm, d_attn), _row_spec(tm, d_conv),
            pl.BlockSpec((CONV_HALO, d_conv), lambda i: (jnp.maximum(i * halo_per_tile - 1, 0), 0)),
            _const_spec(cw.shape), _const_spec(cb.shape), _const_spec(lg.shape), _const_spec(lb.shape),
            _const_spec(wa.shape), _const_spec(wc.shape),
        ],
        out_specs=_row_spec(tm, d),
        out_shape=jax.ShapeDtypeStruct((m, d), f32),
        scratch_shapes=[pltpu.VMEM((CONV_HALO + tm, d_conv), f32),
                        pltpu.VMEM((SUBLANES - 1, CONV_HALO + tm - SUBLANES, d_conv), f32),
                        pltpu.VMEM((tm, d_conv), f32)],
        compiler_params=_cparams(("parallel",)),
        name="even_conv_out_proj",
    )(x, attn, glu, glu, cw, cb, lg, lb, wa, wc)


def _even_out_sample_kernel(x_ref, attn_ref, glu_ref, st_ref, cw_ref, cb_ref, lg_ref, lb_ref, wa_ref, wc_ref,
                            o_ref):
    y = cw_ref[CONV_WIDTH - 1:CONV_WIDTH, :] * glu_ref[...]
    for t in range(CONV_WIDTH - 1):
        y = y + cw_ref[t:t + 1, :] * st_ref[t]
    conv = _conv_ln_silu(y, cb_ref[...], lg_ref[...], lb_ref[...]).astype(bf16)
    o_ref[...] = (x_ref[...] + _dot(attn_ref[...].astype(bf16), wa_ref[...]) + _dot(conv, wc_ref[...]))


def _even_out_sample(x, attn, glu, state_t, cw, cb, lg, lb, wa, wc):
    m, d = x.shape
    args = (x, attn, glu, state_t, cw, cb, lg, lb, wa, wc)
    return pl.pallas_call(
        _even_out_sample_kernel,
        grid=(1,),
        in_specs=[_const_spec(a.shape) for a in args],
        out_specs=_const_spec((m, d)),
        out_shape=jax.ShapeDtypeStruct((m, d), f32),
        compiler_params=_cparams(("arbitrary",)),
        name="even_conv_out_proj_sample",
    )(*args)


def _suffix_sums(f):
    lane = lax.broadcasted_iota(jnp.int32, f.shape, 1)
    x = f
    d = 1
    while d < PAGE_SIZE:
        x = x + jnp.where(lane + d < PAGE_SIZE, pltpu.roll(x, PAGE_SIZE - d, 1), 0.0)
        d *= 2
    return x - f


def _own_head_mask(n_heads, d_attn):
    head_of_lane = lax.broadcasted_iota(jnp.int32, (n_heads, d_attn), 1) // HEAD_DIM
    return head_of_lane == lax.broadcasted_iota(jnp.int32, (n_heads, d_attn), 0)


def _group_logits(qbd, kbuf, fbuf, slot, j, carry, pmax, s_ref, *, n_heads, pages):
    n_pages = pmax.shape[1]
    f_all = fbuf[slot].reshape(pages * n_heads, PAGE_SIZE)
    suffix = _suffix_sums(f_all)
    totals = jnp.sum(f_all, axis=1, keepdims=True)
    page_lane = lax.broadcasted_iota(jnp.int32, pmax.shape, 1)
    for p in range(pages):
        rows = slice(p * n_heads, (p + 1) * n_heads)
        s = _dot(qbd, kbuf[slot, p].astype(bf16)) + (carry + suffix[rows])
        logical = n_pages - 1 - (j * pages + p)
        s_ref[:, pl.ds(pl.multiple_of(logical * PAGE_SIZE, PAGE_SIZE), PAGE_SIZE)] = s
        pmax = jnp.where(page_lane == logical, jnp.max(s, axis=1, keepdims=True), pmax)
        carry = carry + totals[rows]
    return carry, pmax


def _normalise_logits(qbd32, k_new, lf_new, pmax, s_ref):
    s_new = jnp.sum(qbd32 * k_new, axis=1, keepdims=True) - lf_new
    m = jnp.maximum(jnp.max(pmax, axis=1, keepdims=True), s_new)
    e = jnp.exp(s_ref[...] - m)
    w = jnp.exp(s_new - m)
    l = jnp.sum(e, axis=1, keepdims=True) + w
    flags = (jnp.max(jnp.exp(pmax - m), axis=0, keepdims=True) > 0.0).astype(jnp.int32)
    return e, w, l, flags


def _fox_sample_scores_kernel(pt_ref, q_ref, kn_ref, lfn_ref, ck_ref, cf_ref,
                              e_ref, w_ref, l_ref, fl_ref, kbuf, fbuf, sems, s_ref, carry_ref, pmax_ref,
                              *, layer, n_heads, pages):
    db, n_pages = pt_ref.shape
    n_groups = n_pages // pages
    n_steps = db * n_groups
    d_attn = q_ref.shape[-1]
    head_of_lane = lax.broadcasted_iota(jnp.int32, (n_heads, d_attn), 1) // HEAD_DIM
    own = head_of_lane == lax.broadcasted_iota(jnp.int32, (n_heads, d_attn), 0)

    def page_copies(t, slot, p):
        b = lax.div(t, n_groups)
        j = t - b * n_groups
        page = pt_ref[b, n_pages - 1 - (j * pages + p)]
        return (pltpu.make_async_copy(ck_ref.at[layer, page], kbuf.at[slot, p], sems.at[0, slot]),
                pltpu.make_async_copy(cf_ref.at[layer, page], fbuf.at[slot, p], sems.at[1, slot]))

    def start_group(t, slot):
        def one(p, c):
            for cp in page_copies(t, slot, p):
                cp.start()
            return c
        lax.fori_loop(0, pages, one, 0)

    def wait_group(t, slot):
        def one(p, c):
            for cp in page_copies(t, slot, p):
                cp.wait()
            return c
        lax.fori_loop(0, pages, one, 0)

    start_group(0, 0)

    def step(t, carry_unused):
        slot = lax.rem(t, 2)
        b = lax.div(t, n_groups)
        j = t - b * n_groups

        @pl.when(t + 1 < n_steps)
        def _():
            start_group(t + 1, 1 - slot)

        q = q_ref[b].astype(f32)
        qbd32 = jnp.where(own, jnp.broadcast_to(q, (n_heads, d_attn)), 0.0)
        qbd = qbd32.astype(bf16)

        @pl.when(j == 0)
        def _():
            carry_ref[...] = jnp.zeros(carry_ref.shape, f32)
            pmax_ref[...] = jnp.full(pmax_ref.shape, NEG, f32)

        wait_group(t, slot)
        carry, pmax = _group_logits(qbd, kbuf, fbuf, slot, j, carry_ref[...], pmax_ref[...], s_ref,
                                    n_heads=n_heads, pages=pages)
        carry_ref[...] = carry
        pmax_ref[...] = pmax

        @pl.when(j == n_groups - 1)
        def _():
            e_ref[b], w_ref[b], l_ref[b], fl_ref[b] = _normalise_logits(qbd32, kn_ref[b], lfn_ref[b], pmax, s_ref)

        return carry_unused

    lax.fori_loop(0, n_steps, step, 0)


def _fox_sample_pv_kernel(pt_ref, fl_ref, e_ref, w_ref, l_ref, vn_ref, cv_ref, o_ref, vbuf, sems, acc_ref,
                          seq_ref, page_ref, *, layer, n_heads, pages):
    db, n_pages = pt_ref.shape
    d_attn = vn_ref.shape[-1]
    head_of_lane = lax.broadcasted_iota(jnp.int32, (1, n_heads, d_attn), 2) // HEAD_DIM
    own = head_of_lane == lax.broadcasted_iota(jnp.int32, (1, n_heads, d_attn), 1)

    def scan_seq(b, c):
        def scan_page(lp, c):
            seq_ref[c] = b
            page_ref[c] = lp
            return c + (fl_ref[b, lp] != 0).astype(jnp.int32)
        return lax.fori_loop(0, n_pages, scan_page, c)
    n_entries = lax.fori_loop(0, db, scan_seq, 0)
    n_groups = lax.div(n_entries + (pages - 1), pages)

    def entry(g, k):
        i = jnp.minimum(g * pages + k, n_entries - 1)
        return seq_ref[i], page_ref[i]

    def page_copies(g, slot):
        copies = []
        for k in range(pages):
            b, lp = entry(g, k)
            copies.append(pltpu.make_async_copy(cv_ref.at[layer, pt_ref[b, lp]], vbuf.at[slot, k], sems.at[slot]))
        return copies

    acc_ref[...] = w_ref[...] * vn_ref[...]

    @pl.when(n_groups > 0)
    def _():
        for cp in page_copies(0, 0):
            cp.start()

    def group(g, carry_unused):
        slot = lax.rem(g, 2)

        @pl.when(g + 1 < n_groups)
        def _():
            for cp in page_copies(g + 1, 1 - slot):
                cp.start()

        for cp in page_copies(g, slot):
            cp.wait()
        seqs, partial = [], []
        for k in range(pages):
            b, lp = entry(g, k)
            e = e_ref[b, :, pl.ds(pl.multiple_of(lp * PAGE_SIZE, PAGE_SIZE), PAGE_SIZE)]
            e = jnp.where(g * pages + k < n_entries, e, 0.0)
            seqs.append(b)
            partial.append(_dot_nt(e.astype(bf16), vbuf[slot, k].astype(bf16)))
        for b, r in zip(seqs, partial):
            acc_ref[b] += r
        return carry_unused

    lax.fori_loop(0, n_groups, group, 0)
    o_ref[...] = jnp.sum(jnp.where(own, acc_ref[...] / l_ref[...], 0.0), axis=1, keepdims=True)


def _fox_sample_scores(page_table, q, k_new, lf_new, cache_k, cache_f_t, layer):
    db, n_pages = page_table.shape
    n_heads = lf_new.shape[1]
    d_attn = q.shape[1]
    n_past = n_pages * PAGE_SIZE
    pages = PAGES_PER_STEP
    assert n_pages % pages == 0
    hbm = pl.BlockSpec(memory_space=pl.ANY)
    q3, kn3 = q.reshape(db, 1, d_attn), k_new.reshape(db, 1, d_attn)
    lfn3 = lf_new.reshape(db, n_heads, 1)

    whole1 = lambda shape: pl.BlockSpec(shape, lambda i, pt: (0,) * len(shape))
    return pl.pallas_call(
        functools.partial(_fox_sample_scores_kernel, layer=layer, n_heads=n_heads, pages=pages),
        grid_spec=pltpu.PrefetchScalarGridSpec(
            num_scalar_prefetch=1,
            grid=(1,),
            in_specs=[whole1(q3.shape), whole1(kn3.shape), whole1(lfn3.shape), hbm, hbm],
            out_specs=(whole1((db, n_heads, n_past)), whole1((db, n_heads, 1)), whole1((db, n_heads, 1)),
                       whole1((db, 1, n_pages))),
            scratch_shapes=[
                pltpu.VMEM((2, pages, d_attn, PAGE_SIZE), f32),
                pltpu.VMEM((2, pages, n_heads, PAGE_SIZE), f32),
                pltpu.SemaphoreType.DMA((2, 2)),
                pltpu.VMEM((n_heads, n_past), f32),
                pltpu.VMEM((n_heads, 1), f32),
                pltpu.VMEM((n_heads, n_pages), f32)],
        ),
        out_shape=(jax.ShapeDtypeStruct((db, n_heads, n_past), f32), jax.ShapeDtypeStruct((db, n_heads, 1), f32),
                   jax.ShapeDtypeStruct((db, n_heads, 1), f32), jax.ShapeDtypeStruct((db, 1, n_pages), jnp.int32)),
        compiler_params=_cparams(("arbitrary",)),
        name="fox_sample_scores",
    )(page_table, q3, kn3, lfn3, cache_k, cache_f_t)


def _fox_sample_pv(page_table, flags, e, w, l, v_new, cache_v, layer):
    db, n_pages = page_table.shape
    n_heads = e.shape[1]
    d_attn = v_new.shape[1]
    hbm = pl.BlockSpec(memory_space=pl.ANY)
    vn3 = v_new.reshape(db, 1, d_attn)
    whole2 = lambda shape: pl.BlockSpec(shape, lambda i, pt, fl: (0,) * len(shape))
    pv_pages = PV_PAGES_PER_STEP
    out = pl.pallas_call(
        functools.partial(_fox_sample_pv_kernel, layer=layer, n_heads=n_heads, pages=pv_pages),
        grid_spec=pltpu.PrefetchScalarGridSpec(
            num_scalar_prefetch=2,
            grid=(1,),
            in_specs=[whole2(e.shape), whole2(w.shape), whole2(l.shape), whole2(vn3.shape), hbm],
            out_specs=whole2((db, 1, d_attn)),
            scratch_shapes=[
                pltpu.VMEM((2, pv_pages, d_attn, PAGE_SIZE), f32),
                pltpu.SemaphoreType.DMA((2,)),
                pltpu.VMEM((db, n_heads, d_attn), f32),
                pltpu.SMEM((db * n_pages,), jnp.int32),
                pltpu.SMEM((db * n_pages,), jnp.int32)],
        ),
        out_shape=jax.ShapeDtypeStruct((db, 1, d_attn), f32),
        compiler_params=_cparams(("arbitrary",)),
        name="fox_sample_pv",
    )(page_table, flags.reshape(db, n_pages), e, w, l, vn3, cache_v)
    return out.reshape(db, d_attn)


def _sgu_kernel(x_ref, g_ref, win_ref, lg_ref, lb_ref, ws_ref, bs_ref, wout_ref, *out_refs, tm, sample):
    d_sgu = wout_ref.shape[0]
    x = x_ref[...]
    h = _rms(x, g_ref[...]).astype(bf16)
    u = _gelu(_dot(h, win_ref[:, 0:d_sgu]))
    v = _gelu(_dot(h, win_ref[:, d_sgu:2 * d_sgu]))
    vn = _layernorm(v, lg_ref[...], lb_ref[...])
    if sample:
        o_ref, vn_ref = out_refs
        vn_ref[...] = vn
        gated = u * (vn * ws_ref[...] + bs_ref[...])
    else:
        (o_ref,) = out_refs
        vnb = vn.astype(bf16)
        group = d_sgu // N_SGU_GROUPS
        rows = []
        for r in range(tm // CHUNK):
            rs = slice(r * CHUNK, (r + 1) * CHUNK)
            cols = [_dot(ws_ref[g], vnb[rs, g * group:(g + 1) * group]) for g in range(N_SGU_GROUPS)]
            rows.append(jnp.concatenate(cols, axis=1) + bs_ref[...])
        mixed = jnp.concatenate(rows, axis=0)
        gated = u * mixed
    o_ref[...] = x + _dot(gated.astype(bf16), wout_ref[...])


def _sgu(x, g, win, lg, lb, ws, bs, wout, *, tm, sample):
    m, d = x.shape
    d_sgu = wout.shape[0]
    kern = functools.partial(_sgu_kernel, tm=tm, sample=sample)
    out_shape = [jax.ShapeDtypeStruct((m, d), f32)]
    out_specs = [_row_spec(tm, d)]
    if sample:
        out_shape.append(jax.ShapeDtypeStruct((m, d_sgu), f32))
        out_specs.append(_row_spec(tm, d_sgu))
    outs = pl.pallas_call(
        kern,
        grid=(m // tm,),
        in_specs=[_row_spec(tm, d), _const_spec(g.shape), _const_spec(win.shape), _const_spec(lg.shape),
                  _const_spec(lb.shape), _const_spec(ws.shape), _const_spec(bs.shape), _const_spec(wout.shape)],
        out_specs=tuple(out_specs),
        out_shape=tuple(out_shape),
        compiler_params=_cparams(("parallel",)),
        name="sgu_mixer_sample" if sample else "sgu_mixer",
    )(x, g, win, lg, lb, ws, bs, wout)
    return outs if sample else outs[0]


def _ffn_kernel(x_ref, g_ref, wg_ref, wu_ref, wd_ref, gf_ref, o_ref, *, tf, final):
    x = x_ref[...]
    h = _rms(x, g_ref[...]).astype(bf16)
    acc = x
    d_ff = wg_ref.shape[1]
    for c in range(d_ff // tf):
        sl = slice(c * tf, (c + 1) * tf)
        a = _silu(_dot(h, wg_ref[:, sl])) * _dot(h, wu_ref[:, sl])
        acc = acc + _dot(a.astype(bf16), wd_ref[sl, :])
    if final:
        acc = _rms(acc, gf_ref[...])
    o_ref[...] = acc


def _ffn(x, g, wg, wu, wd, gf, *, tm, final):
    m, d = x.shape
    d_ff = wg.shape[1]
    tf = d_ff // 2 if (d_ff // 2) % LANES == 0 else d_ff
    kern = functools.partial(_ffn_kernel, tf=tf, final=final)
    return pl.pallas_call(
        kern,
        grid=(m // tm,),
        in_specs=[_row_spec(tm, d), _const_spec(g.shape), _const_spec(wg.shape), _const_spec(wu.shape),
                  _const_spec(wd.shape), _const_spec(gf.shape)],
        out_specs=_row_spec(tm, d),
        out_shape=jax.ShapeDtypeStruct((m, d), f32),
        compiler_params=_cparams(("parallel",)),
        name="swiglu_ffn_final" if final else "swiglu_ffn",
    )(x, g, wg, wu, wd, gf)


def _ffn_scores_kernel(pt_ref, x_ref, g_ref, wg_ref, wu_ref, wd_ref, q_ref, kn_ref, lfn_ref, ck_ref, cf_ref,
                       o_ref, e_ref, w_ref, l_ref, fl_ref, kbuf, fbuf, sems, s_ref,
                       *, layer, n_heads, pages, chunk_bounds):
    i = pl.program_id(0)
    n_seq, n_pages = pt_ref.shape
    n_groups = n_pages // pages
    d_attn = q_ref.shape[-1]

    def page_copies(b, j, slot, p):
        page = pt_ref[b, n_pages - 1 - (j * pages + p)]
        return (pltpu.make_async_copy(ck_ref.at[layer, page], kbuf.at[slot, p], sems.at[0, slot]),
                pltpu.make_async_copy(cf_ref.at[layer, page], fbuf.at[slot, p], sems.at[1, slot]))

    def start_group(b, j, slot):
        def one(p, c):
            for cp in page_copies(b, j, slot, p):
                cp.start()
            return c
        lax.fori_loop(0, pages, one, 0)

    def wait_group(b, j, slot):
        def one(p, c):
            for cp in page_copies(b, j, slot, p):
                cp.wait()
            return c
        lax.fori_loop(0, pages, one, 0)

    @pl.when(i == 0)
    def _():
        start_group(0, 0, 0)

    x = x_ref[...]
    h = _rms(x, g_ref[...]).astype(bf16)
    acc = x
    q = q_ref[...].astype(f32)
    qbd32 = jnp.where(_own_head_mask(n_heads, d_attn), jnp.broadcast_to(q, (n_heads, d_attn)), 0.0)
    qbd = qbd32.astype(bf16)
    carry = jnp.zeros((n_heads, 1), f32)
    pmax = jnp.full((n_heads, n_pages), NEG, f32)
    next_seq = jnp.minimum(i + 1, n_seq - 1)
    for j in range(n_groups):
        if j + 1 < n_groups:
            start_group(i, j + 1, (j + 1) % 2)
        else:
            start_group(next_seq, 0, 0)
        wait_group(i, j, j % 2)
        carry, pmax = _group_logits(qbd, kbuf, fbuf, j % 2, j, carry, pmax, s_ref, n_heads=n_heads, pages=pages)
        sl = slice(chunk_bounds[j], chunk_bounds[j + 1])
        a = _silu(_dot(h, wg_ref[:, sl])) * _dot(h, wu_ref[:, sl])
        acc = acc + _dot(a.astype(bf16), wd_ref[sl, :])
    o_ref[...] = acc
    e_ref[...], w_ref[...], l_ref[...], fl_ref[...] = _normalise_logits(qbd32, kn_ref[...], lfn_ref[...], pmax, s_ref)

    @pl.when(i == n_seq - 1)
    def _():
        wait_group(next_seq, 0, 0)


def _ffn_with_scores(x, g, wg, wu, wd, page_table, q, k_new, lf_new, cache_k, cache_f_t, layer, *, tm):
    m, d = x.shape
    d_ff = wg.shape[1]
    db, n_pages = page_table.shape
    n_heads = lf_new.shape[1]
    d_attn = q.shape[1]
    n_past = n_pages * PAGE_SIZE
    pages = PAGES_PER_STEP
    n_groups = n_pages // pages
    assert m // tm == db and n_pages % pages == 0 and n_groups % 2 == 0
    col_tile = 2 * LANES
    n_col = d_ff // col_tile
    assert d_ff % col_tile == 0 and n_col >= n_groups
    chunk_bounds = tuple(col_tile * ((n_col * c) // n_groups) for c in range(n_groups + 1))
    hbm = pl.BlockSpec(memory_space=pl.ANY)
    const = lambda a: pl.BlockSpec(a.shape, lambda i, pt: (0,) * a.ndim, pipeline_mode=pl.Buffered(1))
    per_seq = lambda *tail: pl.BlockSpec((None,) + tail, lambda i, pt: (i,) + (0,) * len(tail))
    kern = functools.partial(_ffn_scores_kernel, layer=layer, n_heads=n_heads, pages=pages,
                             chunk_bounds=chunk_bounds)
    return pl.pallas_call(
        kern,
        grid_spec=pltpu.PrefetchScalarGridSpec(
            num_scalar_prefetch=1,
            grid=(db,),
            in_specs=[pl.BlockSpec((tm, d), lambda i, pt: (i, 0)), const(g), const(wg), const(wu), const(wd),
                      per_seq(1, d_attn), per_seq(1, d_attn), per_seq(n_heads, 1), hbm, hbm],
            out_specs=(pl.BlockSpec((tm, d), lambda i, pt: (i, 0)), per_seq(n_heads, n_past), per_seq(n_heads, 1),
                       per_seq(n_heads, 1), per_seq(1, n_pages)),
            scratch_shapes=[
                pltpu.VMEM((2, pages, d_attn, PAGE_SIZE), f32),
                pltpu.VMEM((2, pages, n_heads, PAGE_SIZE), f32),
                pltpu.SemaphoreType.DMA((2, 2)),
                pltpu.VMEM((n_heads, n_past), f32)],
        ),
        out_shape=(jax.ShapeDtypeStruct((m, d), f32),
                   jax.ShapeDtypeStruct((db, n_heads, n_past), f32), jax.ShapeDtypeStruct((db, n_heads, 1), f32),
                   jax.ShapeDtypeStruct((db, n_heads, 1), f32), jax.ShapeDtypeStruct((db, 1, n_pages), jnp.int32)),
        compiler_params=_cparams(("arbitrary",)),
        name="swiglu_ffn_with_sample_scores",
    )(page_table, x, g, wg, wu, wd, q.reshape(db, 1, d_attn), k_new.reshape(db, 1, d_attn),
      lf_new.reshape(db, n_heads, 1), cache_k, cache_f_t)


def kernel(x_prompt, x_sample, cache_k, cache_v, cache_logf, page_table, state_conv, norm_mix, norm_ffn, norm_final, w_in_even, b_forget, conv_w, conv_b, conv_ln_g, conv_ln_b, w_out_even, w_in_odd, sgu_ln_g, sgu_ln_b, sgu_w, sgu_b, w_out_odd, w_gate, w_up, w_down):
    batch, seq, d_model = x_prompt.shape
    db, dec_seq, _ = x_sample.shape
    assert dec_seq == 1, "the sample path handles one new token per sequence"
    depth = norm_mix.shape[0]
    n_heads = b_forget.shape[1]
    d_attn = n_heads * HEAD_DIM
    d_conv = conv_w.shape[2]
    n_phys = cache_k.shape[1]
    n_tok = batch * seq
    row = lambda a: a.reshape(1, -1)

    xp = x_prompt.reshape(n_tok, d_model)
    xs = x_sample.reshape(db, d_model)
    tm_p = ROW_TILE
    tm_s = db

    feature_major = lambda a: jnp.transpose(a, (0, 1, 3, 4, 2)).reshape(a.shape[0], n_phys, d_attn, PAGE_SIZE)
    cache_kt = feature_major(cache_k)
    cache_vt = feature_major(cache_v)
    cache_f_t = jnp.swapaxes(cache_logf.astype(f32), 2, 3)

    kp_l, vp_l, fp_l, cp_l = [], [], [], []
    ks_l, vs_l, fs_l, cs_l = [], [], [], []
    sgu_l = []
    for l in range(depth):
        g_mix = row(norm_mix[l])
        last = l == depth - 1
        if l % 2 == 0:
            i = l // 2
            w_in = w_in_even[i]
            w_in_t = w_in.T
            wq = w_in[:, 0:d_attn].astype(bf16)
            wkt = w_in_t[d_attn:2 * d_attn].astype(bf16)
            wvt = w_in_t[2 * d_attn:3 * d_attn].astype(bf16)
            o = 3 * d_attn
            wft = w_in_t[o:o + n_heads].astype(bf16)
            wglu = w_in[:, o + n_heads:].astype(bf16)
            wa = w_out_even[i][0:d_attn].astype(bf16)
            wc = w_out_even[i][d_attn:].astype(bf16)
            cw, cb, lg, lb = conv_w[i], row(conv_b[i]), row(conv_ln_g[i]), row(conv_ln_b[i])

            qb, ktb, vtb, kt, vt, lf_t, glu = _even_in_prompt(
                xp, g_mix, wq, wkt, wvt, wft, wglu, b_forget[i].reshape(n_heads, 1), tm=tm_p, batch=batch, seq=seq)
            c = _cumsum_lanes(lf_t)
            attn = _fox_prompt(qb, ktb, vtb, c, batch=batch, seq=seq)
            xp = _even_out(xp, attn, glu, cw, cb, lg, lb, wa, wc, tm=tm_p, seq=seq)
            token_major = lambda a: jnp.transpose(a.reshape(batch, n_heads, HEAD_DIM, seq), (0, 3, 1, 2))
            kp_l.append(token_major(kt))
            vp_l.append(token_major(vt))
            fp_l.append(jnp.swapaxes(lf_t, 1, 2))
            cp_l.append(glu.reshape(batch, seq, d_conv)[:, seq - (CONV_WIDTH - 1):])

            qb2, k2, v2, lf2, glu2 = _even_in_sample(xs, g_mix, wq, wkt, wvt, wft, wglu, row(b_forget[i]))
            state_t = jnp.swapaxes(state_conv[i], 0, 1)

            def finish_sample_mixer(xs, scores, v2=v2, glu2=glu2, state_t=state_t, i=i,
                                    cw=cw, cb=cb, lg=lg, lb=lb, wa=wa, wc=wc):
                e, w, l_sum, flags = scores
                attn2 = _fox_sample_pv(page_table, flags.reshape(db, -1), e, w, l_sum, v2, cache_vt, i)
                return _even_out_sample(xs, attn2, glu2, state_t, cw, cb, lg, lb, wa, wc)

            n_groups = page_table.shape[1] // PAGES_PER_STEP
            ride_along = (n_tok // tm_p == db and page_table.shape[1] % PAGES_PER_STEP == 0 and n_groups % 2 == 0
                          and not last)
            if not ride_along:
                xs = finish_sample_mixer(
                    xs, _fox_sample_scores(page_table, qb2, k2, lf2, cache_kt, cache_f_t, i))
            ks_l.append(k2.reshape(db, 1, n_heads, HEAD_DIM))
            vs_l.append(v2.reshape(db, 1, n_heads, HEAD_DIM))
            fs_l.append(lf2.reshape(db, 1, n_heads))
            cs_l.append(jnp.concatenate([state_conv[i][:, 1:], glu2[:, None, :]], axis=1))
        else:
            ride_along = False
            j = l // 2
            d_sgu = w_out_odd.shape[1]
            group = d_sgu // N_SGU_GROUPS
            win = w_in_odd[j].astype(bf16)
            wout = w_out_odd[j].astype(bf16)
            lg, lb = row(sgu_ln_g[j]), row(sgu_ln_b[j])
            causal = jnp.tril(jnp.ones((CHUNK, CHUNK), dtype=bool))
            ws = jnp.where(causal, sgu_w[j], 0).astype(bf16)
            bs = jnp.repeat(jnp.swapaxes(sgu_b[j], 0, 1), group, axis=1)
            xp = _sgu(xp, g_mix, win, lg, lb, ws, bs, wout, tm=tm_p, sample=False)
            ws0 = row(jnp.repeat(sgu_w[j][:, 0, 0], group))
            bs0 = row(jnp.repeat(sgu_b[j][:, 0], group))
            xs, vn_s = _sgu(xs, g_mix, win, lg, lb, ws0, bs0, wout, tm=tm_s, sample=True)
            sgu_l.append(vn_s.reshape(db, 1, d_sgu))
        g_ffn = row(norm_ffn[l])
        gf = row(norm_final)
        wg, wu, wd = w_gate[l].astype(bf16), w_up[l].astype(bf16), w_down[l].astype(bf16)
        if ride_along:
            xp, *scores = _ffn_with_scores(xp, g_ffn, wg, wu, wd, page_table, qb2, k2, lf2, cache_kt, cache_f_t,
                                           l // 2, tm=tm_p)
            xs = finish_sample_mixer(xs, scores)
        else:
            xp = _ffn(xp, g_ffn, wg, wu, wd, gf, tm=FFN_ROW_TILE, final=last)
        xs = _ffn(xs, g_ffn, wg, wu, wd, gf, tm=tm_s, final=last)

    return (xp.reshape(batch, seq, d_model), xs.reshape(db, 1, d_model),
            jnp.stack(kp_l), jnp.stack(vp_l), jnp.stack(fp_l), jnp.stack(cp_l),
            jnp.stack(ks_l), jnp.stack(vs_l), jnp.stack(fs_l), jnp.stack(cs_l),
            jnp.stack(sgu_l))
```

```python
import functools

import jax
import jax.numpy as jnp
from jax import lax
from jax.experimental import pallas as pl
from jax.experimental.pallas import tpu as pltpu

HEAD_DIM = 64
CONV_WIDTH = 31
CHUNK = 128
N_SGU_GROUPS = 8
PAGE_SIZE = 128
RMS_EPS = 1e-6
LN_EPS = 1e-5
LOG2_E = 1.4426950408889634

LANES = 128
SUBLANES = 8
VMEM_LIMIT = 56 * 1024 * 1024

ROW_TILE = 512
FFN_ROW_TILE = 1024
ATTN_TQ = 512
ATTN_TK = 512
CONV_HALO = 32
CONV_ROWS = 64
PAGES_PER_STEP = 32
PV_PAGES_PER_STEP = 8
NEG = -1e30

bf16 = jnp.bfloat16
f32 = jnp.float32


def _cparams(sem):
    return pltpu.CompilerParams(dimension_semantics=sem, vmem_limit_bytes=VMEM_LIMIT)


def _dot(a, b):
    return jnp.dot(a, b, preferred_element_type=f32)


def _dot_nt(a, b):
    return lax.dot_general(a, b, (((1,), (1,)), ((), ())), preferred_element_type=f32)


def _rms(x, g):
    ms = jnp.mean(x * x, axis=-1, keepdims=True)
    return x * lax.rsqrt(ms + RMS_EPS) * g


def _layernorm(x, g, b):
    mu = jnp.mean(x, axis=-1, keepdims=True)
    xc = x - mu
    var = jnp.mean(xc * xc, axis=-1, keepdims=True)
    return xc * lax.rsqrt(var + LN_EPS) * g + b


def _silu(x):
    return x * jax.nn.sigmoid(x)


def _gelu(x):
    return 0.5 * x * (1.0 + lax.erf(x * (2.0 ** -0.5)))


def _log_sigmoid(z):
    return jnp.minimum(z, 0.0) - jnp.log1p(jnp.exp(-jnp.abs(z)))


def _const_spec(shape):
    nd = len(shape)
    return pl.BlockSpec(shape, lambda *_: (0,) * nd, pipeline_mode=pl.Buffered(1))


def _row_spec(tm, width):
    return pl.BlockSpec((tm, width), lambda i: (i, 0))


def _even_in_prompt_kernel(x_ref, g_ref, wq_ref, wkt_ref, wvt_ref, wft_ref, wglu_ref, bf_ref,
                           q_ref, ktb_ref, vtb_ref, kt_ref, vt_ref, lft_ref, glu_ref, *, d_conv):
    h = _rms(x_ref[...], g_ref[...]).astype(bf16)
    q_ref[...] = (_dot(h, wq_ref[...]) * (HEAD_DIM ** -0.5 * LOG2_E)).astype(bf16)
    kt = _dot_nt(wkt_ref[...], h)
    kt_ref[...] = kt
    ktb_ref[...] = kt.astype(bf16)
    vt = _dot_nt(wvt_ref[...], h)
    vt_ref[...] = vt
    vtb_ref[...] = vt.astype(bf16)
    lft_ref[...] = _log_sigmoid(_dot_nt(wft_ref[...], h) + bf_ref[...])
    a = _dot(h, wglu_ref[:, 0:d_conv])
    b = _dot(h, wglu_ref[:, d_conv:2 * d_conv])
    glu_ref[...] = a * jax.nn.sigmoid(b)


def _even_in_prompt(x, g, wq, wkt, wvt, wft, wglu, bf_col, *, tm, batch, seq):
    m, d = x.shape
    d_attn = wq.shape[1]
    n_heads = wft.shape[0]
    d_conv = wglu.shape[1] // 2
    tiles_per_seq = seq // tm
    kern = functools.partial(_even_in_prompt_kernel, d_conv=d_conv)
    t_spec = lambda rows: pl.BlockSpec((None, rows, tm), lambda i: (i // tiles_per_seq, 0, i % tiles_per_seq))
    out_shape = (
        jax.ShapeDtypeStruct((m, d_attn), bf16),
        jax.ShapeDtypeStruct((batch, d_attn, seq), bf16),
        jax.ShapeDtypeStruct((batch, d_attn, seq), bf16),
        jax.ShapeDtypeStruct((batch, d_attn, seq), f32),
        jax.ShapeDtypeStruct((batch, d_attn, seq), f32),
        jax.ShapeDtypeStruct((batch, n_heads, seq), f32),
        jax.ShapeDtypeStruct((m, d_conv), f32),
    )
    args = (x, g, wq, wkt, wvt, wft, wglu, bf_col)
    return pl.pallas_call(
        kern,
        grid=(m // tm,),
        in_specs=[_row_spec(tm, d)] + [_const_spec(a.shape) for a in args[1:]],
        out_specs=(_row_spec(tm, d_attn), t_spec(d_attn), t_spec(d_attn), t_spec(d_attn), t_spec(d_attn),
                   t_spec(n_heads), _row_spec(tm, d_conv)),
        out_shape=out_shape,
        compiler_params=_cparams(("parallel",)),
        name="even_in_proj",
    )(*args)


def _even_in_sample_kernel(x_ref, g_ref, wq_ref, wkt_ref, wvt_ref, wft_ref, wglu_ref, bf_ref,
                           q_ref, k_ref, v_ref, lf_ref, glu_ref, *, d_conv):
    h = _rms(x_ref[...], g_ref[...]).astype(bf16)
    q_ref[...] = (_dot(h, wq_ref[...]) * (HEAD_DIM ** -0.5)).astype(bf16)
    k_ref[...] = _dot_nt(h, wkt_ref[...])
    v_ref[...] = _dot_nt(h, wvt_ref[...])
    lf_ref[...] = _log_sigmoid(_dot_nt(h, wft_ref[...]) + bf_ref[...])
    a = _dot(h, wglu_ref[:, 0:d_conv])
    b = _dot(h, wglu_ref[:, d_conv:2 * d_conv])
    glu_ref[...] = a * jax.nn.sigmoid(b)


def _even_in_sample(x, g, wq, wkt, wvt, wft, wglu, bf_row):
    m, d = x.shape
    d_attn = wq.shape[1]
    n_heads = wft.shape[0]
    d_conv = wglu.shape[1] // 2
    kern = functools.partial(_even_in_sample_kernel, d_conv=d_conv)
    out_shape = (
        jax.ShapeDtypeStruct((m, d_attn), bf16),
        jax.ShapeDtypeStruct((m, d_attn), f32),
        jax.ShapeDtypeStruct((m, d_attn), f32),
        jax.ShapeDtypeStruct((m, n_heads), f32),
        jax.ShapeDtypeStruct((m, d_conv), f32),
    )
    args = (x, g, wq, wkt, wvt, wft, wglu, bf_row)
    return pl.pallas_call(
        kern,
        grid=(1,),
        in_specs=[_const_spec(a.shape) for a in args],
        out_specs=tuple(_const_spec(o.shape) for o in out_shape),
        out_shape=out_shape,
        compiler_params=_cparams(("arbitrary",)),
        name="even_in_proj_sample",
    )(*args)


def _cumsum_kernel(f_ref, c_ref):
    x = f_ref[...]
    n = x.shape[-1]
    lane = lax.broadcasted_iota(jnp.int32, x.shape, 1)
    d = 1
    while d < n:
        x = x + jnp.where(lane >= d, pltpu.roll(x, d, 1), 0.0)
        d *= 2
    c_ref[...] = x * LOG2_E


def _cumsum_lanes(f_t):
    b, h, s = f_t.shape
    return pl.pallas_call(
        _cumsum_kernel,
        grid=(b,),
        in_specs=[pl.BlockSpec((None, h, s), lambda i: (i, 0, 0))],
        out_specs=pl.BlockSpec((None, h, s), lambda i: (i, 0, 0)),
        out_shape=jax.ShapeDtypeStruct((b, h, s), f32),
        compiler_params=_cparams(("parallel",)),
        name="logf_cumsum",
    )(f_t)


def _fox_prompt_kernel(q_ref, k_ref, v_ref, c_ref, o_ref, m_ref, acc_ref, *, tq, tk, n_heads):
    qi = pl.program_id(1)
    pair = 2 * HEAD_DIM
    lane = lax.broadcasted_iota(jnp.int32, (tq, pair), 1)
    first = lane < HEAD_DIM
    vrow = lax.broadcasted_iota(jnp.int32, (pair, tk), 0)
    sum_lane = (HEAD_DIM, 0)
    own_rows = (vrow < HEAD_DIM, vrow >= HEAD_DIM)
    keep = [jnp.where(own_rows[i], 1.0, 0.0).astype(bf16) for i in range(2)]
    ones_row = [jnp.where(vrow == sum_lane[i], 1.0, 0.0).astype(bf16) for i in range(2)]

    qm = []
    for h in range(n_heads):
        q = q_ref[:, (h // 2) * pair:(h // 2 + 1) * pair]
        zero = jnp.zeros_like(q)
        qm.append(jnp.where(first, q, zero) if h % 2 == 0 else jnp.where(first, zero, q))

    m_ref[...] = jnp.full(m_ref.shape, NEG, f32)
    acc_ref[...] = jnp.zeros(acc_ref.shape, f32)

    def block(j, masked):
        start = pl.multiple_of(j * tk, tk)
        for h in range(n_heads):
            rows = slice((h // 2) * pair, (h // 2 + 1) * pair)
            kb = k_ref[rows, pl.ds(start, tk)]
            vb = v_ref[rows, pl.ds(start, tk)]
            va = vb * keep[h % 2] + ones_row[h % 2]
            s = _dot(qm[h], kb) - c_ref[h:h + 1, pl.ds(start, tk)]
            if masked:
                row = lax.broadcasted_iota(jnp.int32, (tq, tk), 0)
                col = lax.broadcasted_iota(jnp.int32, (tq, tk), 1)
                s = jnp.where(col <= row, s, NEG)
            m_prev = m_ref[h]
            m_new = jnp.maximum(m_prev, jnp.max(s, axis=1, keepdims=True))
            p = jnp.exp2(s - jnp.concatenate([m_new] * (tk // pair), axis=1))
            m_ref[h] = m_new
            acc_ref[h] = jnp.exp2(m_prev - m_new) * acc_ref[h] + _dot_nt(p.astype(bf16), va)

    def body(j, carry):
        block(j, False)
        return carry

    n_full = qi * (tq // tk)
    lax.fori_loop(0, n_full, body, 0)
    block(n_full, True)
    for h in range(0, n_heads, 2):
        a0, a1 = acc_ref[h], acc_ref[h + 1]
        o0 = a0 / a0[:, sum_lane[0]:sum_lane[0] + 1]
        o1 = a1 / a1[:, sum_lane[1]:sum_lane[1] + 1]
        o_ref[:, (h // 2) * pair:(h // 2 + 1) * pair] = jnp.where(first, o0, o1).astype(o_ref.dtype)


def _fox_prompt(q, kt, vt, c, *, batch, seq):
    n, d_attn = q.shape
    n_heads = c.shape[1]
    tq, tk = ATTN_TQ, ATTN_TK
    assert tq == tk and seq % tq == 0
    assert 2 * HEAD_DIM == LANES and n_heads % 2 == 0, "one 128-wide block holds exactly two heads"
    nq = seq // tq
    kern = functools.partial(_fox_prompt_kernel, tq=tq, tk=tk, n_heads=n_heads)
    return pl.pallas_call(
        kern,
        grid=(batch, nq),
        in_specs=[
            pl.BlockSpec((tq, d_attn), lambda b, i: (b * nq + i, 0)),
            pl.BlockSpec((None, d_attn, seq), lambda b, i: (b, 0, 0)),
            pl.BlockSpec((None, d_attn, seq), lambda b, i: (b, 0, 0)),
            pl.BlockSpec((None, n_heads, seq), lambda b, i: (b, 0, 0)),
        ],
        out_specs=pl.BlockSpec((tq, d_attn), lambda b, i: (b * nq + i, 0)),
        out_shape=jax.ShapeDtypeStruct((n, d_attn), bf16),
        scratch_shapes=[pltpu.VMEM((n_heads, tq, LANES), f32), pltpu.VMEM((n_heads, tq, LANES), f32)],
        compiler_params=_cparams(("parallel", "arbitrary")),
        name="fox_prompt_attention",
    )(q, kt, vt, c)


def _conv_ln_silu(y, cb, lg, lb):
    return _silu(_layernorm(y + cb, lg, lb))


def _even_out_kernel(x_ref, attn_ref, glu_ref, halo_ref, cw_ref, cb_ref, lg_ref, lb_ref, wa_ref, wc_ref,
                     o_ref, ext_ref, sh_ref, conv_ref, *, tm, tiles_per_seq):
    i = pl.program_id(0)
    starts_sequence = (i % tiles_per_seq) == 0
    ext_ref[0:CONV_HALO, :] = jnp.where(starts_sequence, 0.0, halo_ref[...])
    ext_ref[CONV_HALO:CONV_HALO + tm, :] = glu_ref[...]
    d_conv = glu_ref.shape[1]
    off = CONV_HALO - (CONV_WIDTH - 1)
    sh_rows = sh_ref.shape[1]
    for b in range(1, SUBLANES):
        sh_ref[b - 1] = ext_ref[b:b + sh_rows, :]
    for c in range(d_conv // LANES):
        cs = slice(c * LANES, (c + 1) * LANES)
        for r in range(tm // CONV_ROWS):
            acc = jnp.zeros((CONV_ROWS, LANES), f32)
            for t in range(CONV_WIDTH):
                a, b = divmod(off + t, SUBLANES)
                lo = r * CONV_ROWS + a * SUBLANES
                src = ext_ref[lo:lo + CONV_ROWS, cs] if b == 0 else sh_ref[b - 1, lo:lo + CONV_ROWS, cs]
                acc = acc + cw_ref[t:t + 1, cs] * src
            conv_ref[r * CONV_ROWS:(r + 1) * CONV_ROWS, cs] = acc
    conv = _conv_ln_silu(conv_ref[...], cb_ref[...], lg_ref[...], lb_ref[...]).astype(bf16)
    o_ref[...] = x_ref[...] + _dot(attn_ref[...], wa_ref[...]) + _dot(conv, wc_ref[...])


def _even_out(x, attn, glu, cw, cb, lg, lb, wa, wc, *, tm, seq):
    m, d = x.shape
    d_attn = attn.shape[1]
    d_conv = glu.shape[1]
    assert seq % tm == 0 and tm % CONV_HALO == 0
    tiles_per_seq = seq // tm
    halo_per_tile = tm // CONV_HALO
    kern = functools.partial(_even_out_kernel, tm=tm, tiles_per_seq=tiles_per_seq)
    return pl.pallas_call(
        kern,
        grid=(m // tm,),
        in_specs=[
            _row_spec(tm, d), _row_spec(tm, d_attn), _row_spec(tm, d_conv),
            pl.BlockSpec((CONV_HALO, d_conv), lambda i: (jnp.maximum(i * halo_per_tile - 1, 0), 0)),
            _const_spec(cw.shape), _const_spec(cb.shape), _const_spec(lg.shape), _const_spec(lb.shape),
            _const_spec(wa.shape), _const_spec(wc.shape),
        ],
        out_specs=_row_spec(tm, d),
        out_shape=jax.ShapeDtypeStruct((m, d), f32),
        scratch_shapes=[pltpu.VMEM((CONV_HALO + tm, d_conv), f32),
                        pltpu.VMEM((SUBLANES - 1, CONV_HALO + tm - SUBLANES, d_conv), f32),
                        pltpu.VMEM((tm, d_conv), f32)],
        compiler_params=_cparams(("parallel",)),
        name="even_conv_out_proj",
    )(x, attn, glu, glu, cw, cb, lg, lb, wa, wc)


def _even_out_sample_kernel(x_ref, attn_ref, glu_ref, st_ref, cw_ref, cb_ref, lg_ref, lb_ref, wa_ref, wc_ref,
                            o_ref):
    y = cw_ref[CONV_WIDTH - 1:CONV_WIDTH, :] * glu_ref[...]
    for t in range(CONV_WIDTH - 1):
        y = y + cw_ref[t:t + 1, :] * st_ref[t]
    conv = _conv_ln_silu(y, cb_ref[...], lg_ref[...], lb_ref[...]).astype(bf16)
    o_ref[...] = (x_ref[...] + _dot(attn_ref[...].astype(bf16), wa_ref[...]) + _dot(conv, wc_ref[...]))


def _even_out_sample(x, attn, glu, state_t, cw, cb, lg, lb, wa, wc):
    m, d = x.shape
    args = (x, attn, glu, state_t, cw, cb, lg, lb, wa, wc)
    return pl.pallas_call(
        _even_out_sample_kernel,
        grid=(1,),
        in_specs=[_const_spec(a.shape) for a in args],
        out_specs=_const_spec((m, d)),
        out_shape=jax.ShapeDtypeStruct((m, d), f32),
        compiler_params=_cparams(("arbitrary",)),
        name="even_conv_out_proj_sample",
    )(*args)


def _suffix_sums(f):
    lane = lax.broadcasted_iota(jnp.int32, f.shape, 1)
    x = f
    d = 1
    while d < PAGE_SIZE:
        x = x + jnp.where(lane + d < PAGE_SIZE, pltpu.roll(x, PAGE_SIZE - d, 1), 0.0)
        d *= 2
    return x - f


def _own_head_mask(n_heads, d_attn):
    head_of_lane = lax.broadcasted_iota(jnp.int32, (n_heads, d_attn), 1) // HEAD_DIM
    return head_of_lane == lax.broadcasted_iota(jnp.int32, (n_heads, d_attn), 0)


def _group_logits(qbd, kbuf, fbuf, slot, j, carry, pmax, s_ref, *, n_heads, pages):
    n_pages = pmax.shape[1]
    f_all = fbuf[slot].reshape(pages * n_heads, PAGE_SIZE)
    suffix = _suffix_sums(f_all)
    totals = jnp.sum(f_all, axis=1, keepdims=True)
    page_lane = lax.broadcasted_iota(jnp.int32, pmax.shape, 1)
    for p in range(pages):
        rows = slice(p * n_heads, (p + 1) * n_heads)
        s = _dot(qbd, kbuf[slot, p].astype(bf16)) + (carry + suffix[rows])
        logical = n_pages - 1 - (j * pages + p)
        s_ref[:, pl.ds(pl.multiple_of(logical * PAGE_SIZE, PAGE_SIZE), PAGE_SIZE)] = s
        pmax = jnp.where(page_lane == logical, jnp.max(s, axis=1, keepdims=True), pmax)
        carry = carry + totals[rows]
    return carry, pmax


def _normalise_logits(s_new, pmax, s_ref):
    m = jnp.maximum(jnp.max(pmax, axis=1, keepdims=True), s_new)
    e = jnp.exp(s_ref[...] - m)
    w = jnp.exp(s_new - m)
    l = jnp.sum(e, axis=1, keepdims=True) + w
    flags = (jnp.max(jnp.exp(pmax - m), axis=0, keepdims=True) > 0.0).astype(jnp.int32)
    return e, w, l, flags


def _fox_sample_pv_kernel(pt_ref, fl_ref, e_ref, w_ref, l_ref, vn_ref, cv_ref, o_ref, vbuf, sems, acc_ref,
                          seq_ref, page_ref, *, layer, n_heads, pages):
    db, n_pages = pt_ref.shape
    d_attn = vn_ref.shape[-1]
    own = _own_head_mask(n_heads, d_attn)

    def scan_seq(b, c):
        def scan_page(lp, c):
            seq_ref[c] = b
            page_ref[c] = lp
            return c + (fl_ref[b, lp] != 0).astype(jnp.int32)
        return lax.fori_loop(0, n_pages, scan_page, c, unroll=8)
    n_entries = lax.fori_loop(0, db, scan_seq, 0)
    n_groups = lax.div(n_entries + (pages - 1), pages)

    def entry(g, k):
        i = jnp.minimum(g * pages + k, n_entries - 1)
        return seq_ref[i], page_ref[i]

    def page_copies(g, slot):
        copies = []
        for k in range(pages):
            b, lp = entry(g, k)
            copies.append(pltpu.make_async_copy(cv_ref.at[layer, pt_ref[b, lp]], vbuf.at[slot, k], sems.at[slot]))
        return copies

    acc_ref[...] = jnp.zeros(acc_ref.shape, f32)

    @pl.when(n_groups > 0)
    def _():
        for cp in page_copies(0, 0):
            cp.start()

    def group(g, carry_unused):
        slot = lax.rem(g, 2)

        @pl.when(g + 1 < n_groups)
        def _():
            for cp in page_copies(g + 1, 1 - slot):
                cp.start()

        for cp in page_copies(g, slot):
            cp.wait()
        for k in range(pages):
            b, lp = entry(g, k)
            e = e_ref[b, :, pl.ds(pl.multiple_of(lp * PAGE_SIZE, PAGE_SIZE), PAGE_SIZE)]
            e = jnp.where(g * pages + k < n_entries, e, 0.0)
            for h in range(n_heads):
                rows = slice(h * HEAD_DIM, (h + 1) * HEAD_DIM)
                acc_ref[b, rows, :] += vbuf[slot, k, rows, :] * e[h:h + 1, :]
        return carry_unused

    lax.fori_loop(0, n_groups, group, 0)

    ones = jnp.ones((n_heads, PAGE_SIZE), bf16)

    def finish(b, carry_unused):
        r = acc_ref[b]
        hi = r.astype(bf16)
        mid = (r - hi.astype(f32)).astype(bf16)
        lo = (r - hi.astype(f32) - mid.astype(f32)).astype(bf16)
        pv = _dot_nt(ones, hi) + _dot_nt(ones, mid) + _dot_nt(ones, lo)
        out = (pv + w_ref[b] * vn_ref[b]) / l_ref[b]
        o_ref[b] = jnp.sum(jnp.where(own, out, 0.0), axis=0, keepdims=True)
        return carry_unused

    lax.fori_loop(0, db, finish, 0)


def _fox_sample_pv(page_table, flags, e, w, l, v_new, cache_v, layer):
    db, n_pages = page_table.shape
    n_heads = e.shape[1]
    d_attn = v_new.shape[1]
    hbm = pl.BlockSpec(memory_space=pl.ANY)
    vn3 = v_new.reshape(db, 1, d_attn)
    whole2 = lambda shape: pl.BlockSpec(shape, lambda i, pt, fl: (0,) * len(shape))
    pv_pages = PV_PAGES_PER_STEP
    out = pl.pallas_call(
        functools.partial(_fox_sample_pv_kernel, layer=layer, n_heads=n_heads, pages=pv_pages),
        grid_spec=pltpu.PrefetchScalarGridSpec(
            num_scalar_prefetch=2,
            grid=(1,),
            in_specs=[whole2(e.shape), whole2(w.shape), whole2(l.shape), whole2(vn3.shape), hbm],
            out_specs=whole2((db, 1, d_attn)),
            scratch_shapes=[
                pltpu.VMEM((2, pv_pages, d_attn, PAGE_SIZE), f32),
                pltpu.SemaphoreType.DMA((2,)),
                pltpu.VMEM((db, d_attn, PAGE_SIZE), f32),
                pltpu.SMEM((db * n_pages,), jnp.int32),
                pltpu.SMEM((db * n_pages,), jnp.int32)],
        ),
        out_shape=jax.ShapeDtypeStruct((db, 1, d_attn), f32),
        compiler_params=_cparams(("arbitrary",)),
        name="fox_sample_pv",
    )(page_table, flags.reshape(db, n_pages), e, w, l, vn3, cache_v)
    return out.reshape(db, d_attn)


def _sgu_kernel(x_ref, g_ref, win_ref, lg_ref, lb_ref, ws_ref, bs_ref, wout_ref, *out_refs, tm, sample):
    d_sgu = wout_ref.shape[0]
    x = x_ref[...]
    h = _rms(x, g_ref[...]).astype(bf16)
    u = _gelu(_dot(h, win_ref[:, 0:d_sgu]))
    v = _gelu(_dot(h, win_ref[:, d_sgu:2 * d_sgu]))
    vn = _layernorm(v, lg_ref[...], lb_ref[...])
    if sample:
        o_ref, vn_ref = out_refs
        vn_ref[...] = vn
        gated = u * (vn * ws_ref[...] + bs_ref[...])
    else:
        (o_ref,) = out_refs
        vnb = vn.astype(bf16)
        group = d_sgu // N_SGU_GROUPS
        rows = []
        for r in range(tm // CHUNK):
            rs = slice(r * CHUNK, (r + 1) * CHUNK)
            cols = [_dot(ws_ref[g], vnb[rs, g * group:(g + 1) * group]) for g in range(N_SGU_GROUPS)]
            rows.append(jnp.concatenate(cols, axis=1) + bs_ref[...])
        mixed = jnp.concatenate(rows, axis=0)
        gated = u * mixed
    o_ref[...] = x + _dot(gated.astype(bf16), wout_ref[...])


def _sgu(x, g, win, lg, lb, ws, bs, wout, *, tm, sample):
    m, d = x.shape
    d_sgu = wout.shape[0]
    kern = functools.partial(_sgu_kernel, tm=tm, sample=sample)
    out_shape = [jax.ShapeDtypeStruct((m, d), f32)]
    out_specs = [_row_spec(tm, d)]
    if sample:
        out_shape.append(jax.ShapeDtypeStruct((m, d_sgu), f32))
        out_specs.append(_row_spec(tm, d_sgu))
    outs = pl.pallas_call(
        kern,
        grid=(m // tm,),
        in_specs=[_row_spec(tm, d), _const_spec(g.shape), _const_spec(win.shape), _const_spec(lg.shape),
                  _const_spec(lb.shape), _const_spec(ws.shape), _const_spec(bs.shape), _const_spec(wout.shape)],
        out_specs=tuple(out_specs),
        out_shape=tuple(out_shape),
        compiler_params=_cparams(("parallel",)),
        name="sgu_mixer_sample" if sample else "sgu_mixer",
    )(x, g, win, lg, lb, ws, bs, wout)
    return outs if sample else outs[0]


def _ffn_kernel(x_ref, g_ref, wg_ref, wu_ref, wd_ref, gf_ref, o_ref, *, tf, final):
    x = x_ref[...]
    h = _rms(x, g_ref[...]).astype(bf16)
    acc = x
    d_ff = wg_ref.shape[1]
    for c in range(d_ff // tf):
        sl = slice(c * tf, (c + 1) * tf)
        a = _silu(_dot(h, wg_ref[:, sl])) * _dot(h, wu_ref[:, sl])
        acc = acc + _dot(a.astype(bf16), wd_ref[sl, :])
    if final:
        acc = _rms(acc, gf_ref[...])
    o_ref[...] = acc


def _ffn(x, g, wg, wu, wd, gf, *, tm, final):
    m, d = x.shape
    d_ff = wg.shape[1]
    tf = d_ff // 2 if (d_ff // 2) % LANES == 0 else d_ff
    kern = functools.partial(_ffn_kernel, tf=tf, final=final)
    return pl.pallas_call(
        kern,
        grid=(m // tm,),
        in_specs=[_row_spec(tm, d), _const_spec(g.shape), _const_spec(wg.shape), _const_spec(wu.shape),
                  _const_spec(wd.shape), _const_spec(gf.shape)],
        out_specs=_row_spec(tm, d),
        out_shape=jax.ShapeDtypeStruct((m, d), f32),
        compiler_params=_cparams(("parallel",)),
        name="swiglu_ffn_final" if final else "swiglu_ffn",
    )(x, g, wg, wu, wd, gf)


def _ffn_scores_kernel(pt_ref, x_ref, g_ref, wg_ref, wu_ref, wd_ref, q_ref, kn_ref, lfn_ref, ck_ref, cf_ref,
                       o_ref, e_ref, w_ref, l_ref, fl_ref, kbuf, fbuf, sems, s_ref,
                       *, layer, n_heads, pages, chunk_bounds):
    i = pl.program_id(0)
    n_seq, n_pages = pt_ref.shape
    n_groups = n_pages // pages
    d_attn = q_ref.shape[-1]

    def page_copies(b, j, slot, p):
        page = pt_ref[b, n_pages - 1 - (j * pages + p)]
        return (pltpu.make_async_copy(ck_ref.at[layer, page], kbuf.at[slot, p], sems.at[0, slot]),
                pltpu.make_async_copy(cf_ref.at[layer, page], fbuf.at[slot, p], sems.at[1, slot]))

    def start_group(b, j, slot):
        def one(p, c):
            for cp in page_copies(b, j, slot, p):
                cp.start()
            return c
        lax.fori_loop(0, pages, one, 0)

    def wait_group(b, j, slot):
        def one(p, c):
            for cp in page_copies(b, j, slot, p):
                cp.wait()
            return c
        lax.fori_loop(0, pages, one, 0)

    @pl.when(i == 0)
    def _():
        start_group(0, 0, 0)

    x = x_ref[...]
    h = _rms(x, g_ref[...]).astype(bf16)
    acc = x
    q = q_ref[...].astype(f32)
    qbd32 = jnp.where(_own_head_mask(n_heads, d_attn), jnp.broadcast_to(q, (n_heads, d_attn)), 0.0)
    qbd = qbd32.astype(bf16)
    s_new = jnp.sum(qbd32 * kn_ref[...], axis=1, keepdims=True) - lfn_ref[...]
    carry = jnp.zeros((n_heads, 1), f32)
    pmax = jnp.full((n_heads, n_pages), NEG, f32)
    next_seq = jnp.minimum(i + 1, n_seq - 1)
    for j in range(n_groups):
        if j + 1 < n_groups:
            start_group(i, j + 1, (j + 1) % 2)
        else:
            start_group(next_seq, 0, 0)
        wait_group(i, j, j % 2)
        carry, pmax = _group_logits(qbd, kbuf, fbuf, j % 2, j, carry, pmax, s_ref, n_heads=n_heads, pages=pages)
        sl = slice(chunk_bounds[j], chunk_bounds[j + 1])
        a = _silu(_dot(h, wg_ref[:, sl])) * _dot(h, wu_ref[:, sl])
        acc = acc + _dot(a.astype(bf16), wd_ref[sl, :])
    o_ref[...] = acc
    e_ref[...], w_ref[...], l_ref[...], fl_ref[...] = _normalise_logits(s_new, pmax, s_ref)

    @pl.when(i == n_seq - 1)
    def _():
        wait_group(next_seq, 0, 0)


def _ffn_with_scores(x, g, wg, wu, wd, page_table, q, k_new, lf_new, cache_k, cache_f_t, layer, *, tm):
    m, d = x.shape
    d_ff = wg.shape[1]
    db, n_pages = page_table.shape
    n_heads = lf_new.shape[1]
    d_attn = q.shape[1]
    n_past = n_pages * PAGE_SIZE
    pages = PAGES_PER_STEP
    n_groups = n_pages // pages
    assert m // tm == db and n_pages % pages == 0 and n_groups % 2 == 0
    col_tile = 2 * LANES
    n_col = d_ff // col_tile
    assert d_ff % col_tile == 0 and n_col >= n_groups
    chunk_bounds = tuple(col_tile * ((n_col * c) // n_groups) for c in range(n_groups + 1))
    hbm = pl.BlockSpec(memory_space=pl.ANY)
    const = lambda a: pl.BlockSpec(a.shape, lambda i, pt: (0,) * a.ndim, pipeline_mode=pl.Buffered(1))
    per_seq = lambda *tail: pl.BlockSpec((None,) + tail, lambda i, pt: (i,) + (0,) * len(tail))
    kern = functools.partial(_ffn_scores_kernel, layer=layer, n_heads=n_heads, pages=pages,
                             chunk_bounds=chunk_bounds)
    return pl.pallas_call(
        kern,
        grid_spec=pltpu.PrefetchScalarGridSpec(
            num_scalar_prefetch=1,
            grid=(db,),
            in_specs=[pl.BlockSpec((tm, d), lambda i, pt: (i, 0)), const(g), const(wg), const(wu), const(wd),
                      per_seq(1, d_attn), per_seq(1, d_attn), per_seq(n_heads, 1), hbm, hbm],
            out_specs=(pl.BlockSpec((tm, d), lambda i, pt: (i, 0)), per_seq(n_heads, n_past), per_seq(n_heads, 1),
                       per_seq(n_heads, 1), per_seq(1, n_pages)),
            scratch_shapes=[
                pltpu.VMEM((2, pages, d_attn, PAGE_SIZE), f32),
                pltpu.VMEM((2, pages, n_heads, PAGE_SIZE), f32),
                pltpu.SemaphoreType.DMA((2, 2)),
                pltpu.VMEM((n_heads, n_past), f32)],
        ),
        out_shape=(jax.ShapeDtypeStruct((m, d), f32),
                   jax.ShapeDtypeStruct((db, n_heads, n_past), f32), jax.ShapeDtypeStruct((db, n_heads, 1), f32),
                   jax.ShapeDtypeStruct((db, n_heads, 1), f32), jax.ShapeDtypeStruct((db, 1, n_pages), jnp.int32)),
        compiler_params=_cparams(("arbitrary",)),
        name="swiglu_ffn_with_sample_scores",
    )(page_table, x, g, wg, wu, wd, q.reshape(db, 1, d_attn), k_new.reshape(db, 1, d_attn),
      lf_new.reshape(db, n_heads, 1), cache_k, cache_f_t)


def kernel(x_prompt, x_sample, cache_k, cache_v, cache_logf, page_table, state_conv, norm_mix, norm_ffn, norm_final, w_in_even, b_forget, conv_w, conv_b, conv_ln_g, conv_ln_b, w_out_even, w_in_odd, sgu_ln_g, sgu_ln_b, sgu_w, sgu_b, w_out_odd, w_gate, w_up, w_down):
    batch, seq, d_model = x_prompt.shape
    db, dec_seq, _ = x_sample.shape
    assert dec_seq == 1, "the sample path handles one new token per sequence"
    depth = norm_mix.shape[0]
    n_heads = b_forget.shape[1]
    d_attn = n_heads * HEAD_DIM
    d_conv = conv_w.shape[2]
    n_phys = cache_k.shape[1]
    n_tok = batch * seq
    row = lambda a: a.reshape(1, -1)

    xp = x_prompt.reshape(n_tok, d_model)
    xs = x_sample.reshape(db, d_model)
    tm_p = ROW_TILE
    tm_s = db

    feature_major = lambda a: jnp.transpose(a, (0, 1, 3, 4, 2)).reshape(a.shape[0], n_phys, d_attn, PAGE_SIZE)
    cache_kt = feature_major(cache_k)
    cache_vt = feature_major(cache_v)
    cache_f_t = jnp.swapaxes(cache_logf.astype(f32), 2, 3)

    kp_l, vp_l, fp_l, cp_l = [], [], [], []
    ks_l, vs_l, fs_l, cs_l = [], [], [], []
    sgu_l = []
    for l in range(depth):
        g_mix = row(norm_mix[l])
        last = l == depth - 1
        if l % 2 == 0:
            i = l // 2
            w_in = w_in_even[i]
            w_in_t = w_in.T
            wq = w_in[:, 0:d_attn].astype(bf16)
            wkt = w_in_t[d_attn:2 * d_attn].astype(bf16)
            wvt = w_in_t[2 * d_attn:3 * d_attn].astype(bf16)
            o = 3 * d_attn
            wft = w_in_t[o:o + n_heads].astype(bf16)
            wglu = w_in[:, o + n_heads:].astype(bf16)
            wa = w_out_even[i][0:d_attn].astype(bf16)
            wc = w_out_even[i][d_attn:].astype(bf16)
            cw, cb, lg, lb = conv_w[i], row(conv_b[i]), row(conv_ln_g[i]), row(conv_ln_b[i])

            qb, ktb, vtb, kt, vt, lf_t, glu = _even_in_prompt(
                xp, g_mix, wq, wkt, wvt, wft, wglu, b_forget[i].reshape(n_heads, 1), tm=tm_p, batch=batch, seq=seq)
            c = _cumsum_lanes(lf_t)
            attn = _fox_prompt(qb, ktb, vtb, c, batch=batch, seq=seq)
            xp = _even_out(xp, attn, glu, cw, cb, lg, lb, wa, wc, tm=tm_p, seq=seq)
            token_major = lambda a: jnp.transpose(a.reshape(batch, n_heads, HEAD_DIM, seq), (0, 3, 1, 2))
            kp_l.append(token_major(kt))
            vp_l.append(token_major(vt))
            fp_l.append(jnp.swapaxes(lf_t, 1, 2))
            cp_l.append(glu.reshape(batch, seq, d_conv)[:, seq - (CONV_WIDTH - 1):])

            qb2, k2, v2, lf2, glu2 = _even_in_sample(xs, g_mix, wq, wkt, wvt, wft, wglu, row(b_forget[i]))
            state_t = jnp.swapaxes(state_conv[i], 0, 1)

            def finish_sample_mixer(xs, scores, v2=v2, glu2=glu2, state_t=state_t, i=i,
                                    cw=cw, cb=cb, lg=lg, lb=lb, wa=wa, wc=wc):
                e, w, l_sum, flags = scores
                attn2 = _fox_sample_pv(page_table, flags.reshape(db, -1), e, w, l_sum, v2, cache_vt, i)
                return _even_out_sample(xs, attn2, glu2, state_t, cw, cb, lg, lb, wa, wc)

            assert not last, "an even layer is followed by an odd one in this trunk"
            ride_along = True
            ks_l.append(k2.reshape(db, 1, n_heads, HEAD_DIM))
            vs_l.append(v2.reshape(db, 1, n_heads, HEAD_DIM))
            fs_l.append(lf2.reshape(db, 1, n_heads))
            cs_l.append(jnp.concatenate([state_conv[i][:, 1:], glu2[:, None, :]], axis=1))
        else:
            ride_along = False
            j = l // 2
            d_sgu = w_out_odd.shape[1]
            group = d_sgu // N_SGU_GROUPS
            win = w_in_odd[j].astype(bf16)
            wout = w_out_odd[j].astype(bf16)
            lg, lb = row(sgu_ln_g[j]), row(sgu_ln_b[j])
            causal = jnp.tril(jnp.ones((CHUNK, CHUNK), dtype=bool))
            ws = jnp.where(causal, sgu_w[j], 0).astype(bf16)
            bs = jnp.repeat(jnp.swapaxes(sgu_b[j], 0, 1), group, axis=1)
            xp = _sgu(xp, g_mix, win, lg, lb, ws, bs, wout, tm=tm_p, sample=False)
            ws0 = row(jnp.repeat(sgu_w[j][:, 0, 0], group))
            bs0 = row(jnp.repeat(sgu_b[j][:, 0], group))
            xs, vn_s = _sgu(xs, g_mix, win, lg, lb, ws0, bs0, wout, tm=tm_s, sample=True)
            sgu_l.append(vn_s.reshape(db, 1, d_sgu))
        g_ffn = row(norm_ffn[l])
        gf = row(norm_final)
        wg, wu, wd = w_gate[l].astype(bf16), w_up[l].astype(bf16), w_down[l].astype(bf16)
        if ride_along:
            xp, *scores = _ffn_with_scores(xp, g_ffn, wg, wu, wd, page_table, qb2, k2, lf2, cache_kt, cache_f_t,
                                           l // 2, tm=tm_p)
            xs = finish_sample_mixer(xs, scores)
        else:
            xp = _ffn(xp, g_ffn, wg, wu, wd, gf, tm=FFN_ROW_TILE, final=last)
        xs = _ffn(xs, g_ffn, wg, wu, wd, gf, tm=tm_s, final=last)

    return (xp.reshape(batch, seq, d_model), xs.reshape(db, 1, d_model),
            jnp.stack(kp_l), jnp.stack(vp_l), jnp.stack(fp_l), jnp.stack(cp_l),
            jnp.stack(ks_l), jnp.stack(vs_l), jnp.stack(fs_l), jnp.stack(cs_l),
            jnp.stack(sgu_l))
```

```python
import functools

import jax
import jax.numpy as jnp
from jax import lax
from jax.experimental import pallas as pl
from jax.experimental.pallas import tpu as pltpu

HEAD_DIM = 64
CONV_WIDTH = 31
CHUNK = 128
N_SGU_GROUPS = 8
PAGE_SIZE = 128
RMS_EPS = 1e-6
LN_EPS = 1e-5
LOG2_E = 1.4426950408889634

LANES = 128
SUBLANES = 8
VMEM_LIMIT = 56 * 1024 * 1024

ROW_TILE = 512
FFN_ROW_TILE = 1024
ATTN_TQ = 512
ATTN_TK = 512
CONV_HALO = 32
CONV_ROWS = 64
PAGES_PER_STEP = 32
PV_PAGES_PER_STEP = 8
PV_SLOTS = 4
NEG = -1e30

bf16 = jnp.bfloat16
f32 = jnp.float32


def _cparams(sem):
    return pltpu.CompilerParams(dimension_semantics=sem, vmem_limit_bytes=VMEM_LIMIT)


def _dot(a, b):
    return jnp.dot(a, b, preferred_element_type=f32)


def _dot_nt(a, b):
    return lax.dot_general(a, b, (((1,), (1,)), ((), ())), preferred_element_type=f32)


def _rms(x, g):
    ms = jnp.mean(x * x, axis=-1, keepdims=True)
    return x * lax.rsqrt(ms + RMS_EPS) * g


def _layernorm(x, g, b):
    mu = jnp.mean(x, axis=-1, keepdims=True)
    xc = x - mu
    var = jnp.mean(xc * xc, axis=-1, keepdims=True)
    return xc * lax.rsqrt(var + LN_EPS) * g + b


def _silu(x):
    return x * jax.nn.sigmoid(x)


def _gelu(x):
    return 0.5 * x * (1.0 + lax.erf(x * (2.0 ** -0.5)))


def _log_sigmoid(z):
    return jnp.minimum(z, 0.0) - jnp.log1p(jnp.exp(-jnp.abs(z)))


def _const_spec(shape):
    nd = len(shape)
    return pl.BlockSpec(shape, lambda *_: (0,) * nd, pipeline_mode=pl.Buffered(1))


def _row_spec(tm, width):
    return pl.BlockSpec((tm, width), lambda i: (i, 0))


def _even_in_prompt_kernel(x_ref, g_ref, wq_ref, wkt_ref, wvt_ref, wft_ref, wglu_ref, bf_ref,
                           q_ref, ktb_ref, vtb_ref, kt_ref, vt_ref, lft_ref, glu_ref, *, d_conv):
    h = _rms(x_ref[...], g_ref[...]).astype(bf16)
    q_ref[...] = (_dot(h, wq_ref[...]) * (HEAD_DIM ** -0.5 * LOG2_E)).astype(bf16)
    kt = _dot_nt(wkt_ref[...], h)
    kt_ref[...] = kt
    ktb_ref[...] = kt.astype(bf16)
    vt = _dot_nt(wvt_ref[...], h)
    vt_ref[...] = vt
    vtb_ref[...] = vt.astype(bf16)
    lft_ref[...] = _log_sigmoid(_dot_nt(wft_ref[...], h) + bf_ref[...])
    a = _dot(h, wglu_ref[:, 0:d_conv])
    b = _dot(h, wglu_ref[:, d_conv:2 * d_conv])
    glu_ref[...] = a * jax.nn.sigmoid(b)


def _even_in_prompt(x, g, wq, wkt, wvt, wft, wglu, bf_col, *, tm, batch, seq):
    m, d = x.shape
    d_attn = wq.shape[1]
    n_heads = wft.shape[0]
    d_conv = wglu.shape[1] // 2
    tiles_per_seq = seq // tm
    kern = functools.partial(_even_in_prompt_kernel, d_conv=d_conv)
    t_spec = lambda rows: pl.BlockSpec((None, rows, tm), lambda i: (i // tiles_per_seq, 0, i % tiles_per_seq))
    out_shape = (
        jax.ShapeDtypeStruct((m, d_attn), bf16),
        jax.ShapeDtypeStruct((batch, d_attn, seq), bf16),
        jax.ShapeDtypeStruct((batch, d_attn, seq), bf16),
        jax.ShapeDtypeStruct((batch, d_attn, seq), f32),
        jax.ShapeDtypeStruct((batch, d_attn, seq), f32),
        jax.ShapeDtypeStruct((batch, n_heads, seq), f32),
        jax.ShapeDtypeStruct((m, d_conv), f32),
    )
    args = (x, g, wq, wkt, wvt, wft, wglu, bf_col)
    return pl.pallas_call(
        kern,
        grid=(m // tm,),
        in_specs=[_row_spec(tm, d)] + [_const_spec(a.shape) for a in args[1:]],
        out_specs=(_row_spec(tm, d_attn), t_spec(d_attn), t_spec(d_attn), t_spec(d_attn), t_spec(d_attn),
                   t_spec(n_heads), _row_spec(tm, d_conv)),
        out_shape=out_shape,
        compiler_params=_cparams(("parallel",)),
        name="even_in_proj",
    )(*args)


def _even_in_sample_kernel(x_ref, g_ref, wq_ref, wkt_ref, wvt_ref, wft_ref, wglu_ref, bf_ref,
                           q_ref, k_ref, v_ref, lf_ref, glu_ref, *, d_conv):
    h = _rms(x_ref[...], g_ref[...]).astype(bf16)
    q_ref[...] = (_dot(h, wq_ref[...]) * (HEAD_DIM ** -0.5)).astype(bf16)
    k_ref[...] = _dot_nt(h, wkt_ref[...])
    v_ref[...] = _dot_nt(h, wvt_ref[...])
    lf_ref[...] = _log_sigmoid(_dot_nt(h, wft_ref[...]) + bf_ref[...])
    a = _dot(h, wglu_ref[:, 0:d_conv])
    b = _dot(h, wglu_ref[:, d_conv:2 * d_conv])
    glu_ref[...] = a * jax.nn.sigmoid(b)


def _even_in_sample(x, g, wq, wkt, wvt, wft, wglu, bf_row):
    m, d = x.shape
    d_attn = wq.shape[1]
    n_heads = wft.shape[0]
    d_conv = wglu.shape[1] // 2
    kern = functools.partial(_even_in_sample_kernel, d_conv=d_conv)
    out_shape = (
        jax.ShapeDtypeStruct((m, d_attn), bf16),
        jax.ShapeDtypeStruct((m, d_attn), f32),
        jax.ShapeDtypeStruct((m, d_attn), f32),
        jax.ShapeDtypeStruct((m, n_heads), f32),
        jax.ShapeDtypeStruct((m, d_conv), f32),
    )
    args = (x, g, wq, wkt, wvt, wft, wglu, bf_row)
    return pl.pallas_call(
        kern,
        grid=(1,),
        in_specs=[_const_spec(a.shape) for a in args],
        out_specs=tuple(_const_spec(o.shape) for o in out_shape),
        out_shape=out_shape,
        compiler_params=_cparams(("arbitrary",)),
        name="even_in_proj_sample",
    )(*args)


def _cumsum_kernel(f_ref, c_ref):
    x = f_ref[...]
    n = x.shape[-1]
    lane = lax.broadcasted_iota(jnp.int32, x.shape, 1)
    d = 1
    while d < n:
        x = x + jnp.where(lane >= d, pltpu.roll(x, d, 1), 0.0)
        d *= 2
    c_ref[...] = x * LOG2_E


def _cumsum_lanes(f_t):
    b, h, s = f_t.shape
    return pl.pallas_call(
        _cumsum_kernel,
        grid=(b,),
        in_specs=[pl.BlockSpec((None, h, s), lambda i: (i, 0, 0))],
        out_specs=pl.BlockSpec((None, h, s), lambda i: (i, 0, 0)),
        out_shape=jax.ShapeDtypeStruct((b, h, s), f32),
        compiler_params=_cparams(("parallel",)),
        name="logf_cumsum",
    )(f_t)


def _fox_prompt_kernel(q_ref, k_ref, v_ref, c_ref, o_ref, m_ref, acc_ref, *, tq, tk, n_heads):
    qi = pl.program_id(1)
    pair = 2 * HEAD_DIM
    lane = lax.broadcasted_iota(jnp.int32, (tq, pair), 1)
    first = lane < HEAD_DIM
    vrow = lax.broadcasted_iota(jnp.int32, (pair, tk), 0)
    sum_lane = (HEAD_DIM, 0)
    own_rows = (vrow < HEAD_DIM, vrow >= HEAD_DIM)
    keep = [jnp.where(own_rows[i], 1.0, 0.0).astype(bf16) for i in range(2)]
    ones_row = [jnp.where(vrow == sum_lane[i], 1.0, 0.0).astype(bf16) for i in range(2)]

    qm = []
    for h in range(n_heads):
        q = q_ref[:, (h // 2) * pair:(h // 2 + 1) * pair]
        zero = jnp.zeros_like(q)
        qm.append(jnp.where(first, q, zero) if h % 2 == 0 else jnp.where(first, zero, q))

    m_ref[...] = jnp.full(m_ref.shape, NEG, f32)
    acc_ref[...] = jnp.zeros(acc_ref.shape, f32)

    def block(j, masked):
        start = pl.multiple_of(j * tk, tk)
        for h in range(n_heads):
            rows = slice((h // 2) * pair, (h // 2 + 1) * pair)
            kb = k_ref[rows, pl.ds(start, tk)]
            vb = v_ref[rows, pl.ds(start, tk)]
            va = vb * keep[h % 2] + ones_row[h % 2]
            s = _dot(qm[h], kb) - c_ref[h:h + 1, pl.ds(start, tk)]
            if masked:
                row = lax.broadcasted_iota(jnp.int32, (tq, tk), 0)
                col = lax.broadcasted_iota(jnp.int32, (tq, tk), 1)
                s = jnp.where(col <= row, s, NEG)
            m_prev = m_ref[h]
            m_new = jnp.maximum(m_prev, jnp.max(s, axis=1, keepdims=True))
            p = jnp.exp2(s - jnp.concatenate([m_new] * (tk // pair), axis=1))
            m_ref[h] = m_new
            acc_ref[h] = jnp.exp2(m_prev - m_new) * acc_ref[h] + _dot_nt(p.astype(bf16), va)

    def body(j, carry):
        block(j, False)
        return carry

    n_full = qi * (tq // tk)
    lax.fori_loop(0, n_full, body, 0)
    block(n_full, True)
    for h in range(0, n_heads, 2):
        a0, a1 = acc_ref[h], acc_ref[h + 1]
        o0 = a0 / a0[:, sum_lane[0]:sum_lane[0] + 1]
        o1 = a1 / a1[:, sum_lane[1]:sum_lane[1] + 1]
        o_ref[:, (h // 2) * pair:(h // 2 + 1) * pair] = jnp.where(first, o0, o1).astype(o_ref.dtype)


def _fox_prompt(q, kt, vt, c, *, batch, seq):
    n, d_attn = q.shape
    n_heads = c.shape[1]
    tq, tk = ATTN_TQ, ATTN_TK
    assert tq == tk and seq % tq == 0
    assert 2 * HEAD_DIM == LANES and n_heads % 2 == 0, "one 128-wide block holds exactly two heads"
    nq = seq // tq
    kern = functools.partial(_fox_prompt_kernel, tq=tq, tk=tk, n_heads=n_heads)
    return pl.pallas_call(
        kern,
        grid=(batch, nq),
        in_specs=[
            pl.BlockSpec((tq, d_attn), lambda b, i: (b * nq + i, 0)),
            pl.BlockSpec((None, d_attn, seq), lambda b, i: (b, 0, 0)),
            pl.BlockSpec((None, d_attn, seq), lambda b, i: (b, 0, 0)),
            pl.BlockSpec((None, n_heads, seq), lambda b, i: (b, 0, 0)),
        ],
        out_specs=pl.BlockSpec((tq, d_attn), lambda b, i: (b * nq + i, 0)),
        out_shape=jax.ShapeDtypeStruct((n, d_attn), bf16),
        scratch_shapes=[pltpu.VMEM((n_heads, tq, LANES), f32), pltpu.VMEM((n_heads, tq, LANES), f32)],
        compiler_params=_cparams(("parallel", "arbitrary")),
        name="fox_prompt_attention",
    )(q, kt, vt, c)


def _conv_ln_silu(y, cb, lg, lb):
    return _silu(_layernorm(y + cb, lg, lb))


def _even_out_kernel(x_ref, attn_ref, glu_ref, halo_ref, cw_ref, cb_ref, lg_ref, lb_ref, wa_ref, wc_ref,
                     o_ref, ext_ref, sh_ref, conv_ref, *, tm, tiles_per_seq):
    i = pl.program_id(0)
    starts_sequence = (i % tiles_per_seq) == 0
    ext_ref[0:CONV_HALO, :] = jnp.where(starts_sequence, 0.0, halo_ref[...])
    ext_ref[CONV_HALO:CONV_HALO + tm, :] = glu_ref[...]
    d_conv = glu_ref.shape[1]
    off = CONV_HALO - (CONV_WIDTH - 1)
    sh_rows = sh_ref.shape[1]
    for b in range(1, SUBLANES):
        sh_ref[b - 1] = ext_ref[b:b + sh_rows, :]
    for c in range(d_conv // LANES):
        cs = slice(c * LANES, (c + 1) * LANES)
        for r in range(tm // CONV_ROWS):
            acc = jnp.zeros((CONV_ROWS, LANES), f32)
            for t in range(CONV_WIDTH):
                a, b = divmod(off + t, SUBLANES)
                lo = r * CONV_ROWS + a * SUBLANES
                src = ext_ref[lo:lo + CONV_ROWS, cs] if b == 0 else sh_ref[b - 1, lo:lo + CONV_ROWS, cs]
                acc = acc + cw_ref[t:t + 1, cs] * src
            conv_ref[r * CONV_ROWS:(r + 1) * CONV_ROWS, cs] = acc
    conv = _conv_ln_silu(conv_ref[...], cb_ref[...], lg_ref[...], lb_ref[...]).astype(bf16)
    o_ref[...] = x_ref[...] + _dot(attn_ref[...], wa_ref[...]) + _dot(conv, wc_ref[...])


def _even_out(x, attn, glu, cw, cb, lg, lb, wa, wc, *, tm, seq):
    m, d = x.shape
    d_attn = attn.shape[1]
    d_conv = glu.shape[1]
    assert seq % tm == 0 and tm % CONV_HALO == 0
    tiles_per_seq = seq // tm
    halo_per_tile = tm // CONV_HALO
    kern = functools.partial(_even_out_kernel, tm=tm, tiles_per_seq=tiles_per_seq)
    return pl.pallas_call(
        kern,
        grid=(m // tm,),
        in_specs=[
            _row_spec(tm, d), _row_spec(tm, d_attn), _row_spec(tm, d_conv),
            pl.BlockSpec((CONV_HALO, d_conv), lambda i: (jnp.maximum(i * halo_per_tile - 1, 0), 0)),
            _const_spec(cw.shape), _const_spec(cb.shape), _const_spec(lg.shape), _const_spec(lb.shape),
            _const_spec(wa.shape), _const_spec(wc.shape),
        ],
        out_specs=_row_spec(tm, d),
        out_shape=jax.ShapeDtypeStruct((m, d), f32),
        scratch_shapes=[pltpu.VMEM((CONV_HALO + tm, d_conv), f32),
                        pltpu.VMEM((SUBLANES - 1, CONV_HALO + tm - SUBLANES, d_conv), f32),
                        pltpu.VMEM((tm, d_conv), f32)],
        compiler_params=_cparams(("parallel",)),
        name="even_conv_out_proj",
    )(x, attn, glu, glu, cw, cb, lg, lb, wa, wc)


def _even_out_sample_kernel(x_ref, attn_ref, glu_ref, st_ref, cw_ref, cb_ref, lg_ref, lb_ref, wa_ref, wc_ref,
                            o_ref):
    y = cw_ref[CONV_WIDTH - 1:CONV_WIDTH, :] * glu_ref[...]
    for t in range(CONV_WIDTH - 1):
        y = y + cw_ref[t:t + 1, :] * st_ref[t]
    conv = _conv_ln_silu(y, cb_ref[...], lg_ref[...], lb_ref[...]).astype(bf16)
    o_ref[...] = (x_ref[...] + _dot(attn_ref[...].astype(bf16), wa_ref[...]) + _dot(conv, wc_ref[...]))


def _even_out_sample(x, attn, glu, state_t, cw, cb, lg, lb, wa, wc):
    m, d = x.shape
    args = (x, attn, glu, state_t, cw, cb, lg, lb, wa, wc)
    return pl.pallas_call(
        _even_out_sample_kernel,
        grid=(1,),
        in_specs=[_const_spec(a.shape) for a in args],
        out_specs=_const_spec((m, d)),
        out_shape=jax.ShapeDtypeStruct((m, d), f32),
        compiler_params=_cparams(("arbitrary",)),
        name="even_conv_out_proj_sample",
    )(*args)


def _suffix_sums(f):
    lane = lax.broadcasted_iota(jnp.int32, f.shape, 1)
    x = f
    d = 1
    while d < PAGE_SIZE:
        x = x + jnp.where(lane + d < PAGE_SIZE, pltpu.roll(x, PAGE_SIZE - d, 1), 0.0)
        d *= 2
    return x - f


def _own_head_mask(n_heads, d_attn):
    head_of_lane = lax.broadcasted_iota(jnp.int32, (n_heads, d_attn), 1) // HEAD_DIM
    return head_of_lane == lax.broadcasted_iota(jnp.int32, (n_heads, d_attn), 0)


def _group_logits(qbd, kbuf, fbuf, slot, j, carry, pmax, s_ref, *, n_heads, pages):
    n_pages = pmax.shape[1]
    f_all = fbuf[slot].reshape(pages * n_heads, PAGE_SIZE)
    suffix = _suffix_sums(f_all)
    totals = jnp.sum(f_all, axis=1, keepdims=True)
    page_lane = lax.broadcasted_iota(jnp.int32, pmax.shape, 1)
    for p in range(pages):
        rows = slice(p * n_heads, (p + 1) * n_heads)
        s = _dot(qbd, kbuf[slot, p].astype(bf16)) + (carry + suffix[rows])
        logical = n_pages - 1 - (j * pages + p)
        s_ref[:, pl.ds(pl.multiple_of(logical * PAGE_SIZE, PAGE_SIZE), PAGE_SIZE)] = s
        pmax = jnp.where(page_lane == logical, jnp.max(s, axis=1, keepdims=True), pmax)
        carry = carry + totals[rows]
    return carry, pmax


def _normalise_logits(s_new, pmax, s_ref):
    m = jnp.maximum(jnp.max(pmax, axis=1, keepdims=True), s_new)
    e = jnp.exp(s_ref[...] - m)
    w = jnp.exp(s_new - m)
    l = jnp.sum(e, axis=1, keepdims=True) + w
    flags = (jnp.max(jnp.exp(pmax - m), axis=0, keepdims=True) > 0.0).astype(jnp.int32)
    return e, w, l, flags


def _fox_sample_pv_kernel(pt_ref, fl_ref, e_ref, w_ref, l_ref, vn_ref, cv_ref, o_ref, vbuf, sems, acc_ref,
                          seq_ref, page_ref, *, layer, n_heads, pages):
    db, n_pages = pt_ref.shape
    d_attn = vn_ref.shape[-1]
    own = _own_head_mask(n_heads, d_attn)

    def scan_seq(b, c):
        def scan_page(lp, c):
            seq_ref[c] = b
            page_ref[c] = lp
            return c + (fl_ref[b, lp] != 0).astype(jnp.int32)
        return lax.fori_loop(0, n_pages, scan_page, c, unroll=8)
    n_entries = lax.fori_loop(0, db, scan_seq, 0)
    n_groups = lax.div(n_entries + (pages - 1), pages)

    def entry(g, k):
        i = jnp.minimum(g * pages + k, n_entries - 1)
        return seq_ref[i], page_ref[i]

    def page_copies(g, slot):
        copies = []
        for k in range(pages):
            b, lp = entry(g, k)
            copies.append(pltpu.make_async_copy(cv_ref.at[layer, pt_ref[b, lp]], vbuf.at[slot, k], sems.at[slot]))
        return copies

    acc_ref[...] = jnp.zeros(acc_ref.shape, f32)

    n_slots = vbuf.shape[0]
    for ahead in range(n_slots - 1):
        @pl.when(ahead < n_groups)
        def _(ahead=ahead):
            for cp in page_copies(ahead, ahead):
                cp.start()

    def group(g, carry_unused):
        slot = lax.rem(g, n_slots)

        @pl.when(g + (n_slots - 1) < n_groups)
        def _():
            for cp in page_copies(g + (n_slots - 1), lax.rem(g + (n_slots - 1), n_slots)):
                cp.start()

        for cp in page_copies(g, slot):
            cp.wait()
        for k in range(pages):
            b, lp = entry(g, k)
            e = e_ref[b, :, pl.ds(pl.multiple_of(lp * PAGE_SIZE, PAGE_SIZE), PAGE_SIZE)]
            e = jnp.where(g * pages + k < n_entries, e, 0.0)
            for h in range(n_heads):
                rows = slice(h * HEAD_DIM, (h + 1) * HEAD_DIM)
                acc_ref[b, rows, :] += vbuf[slot, k, rows, :] * e[h:h + 1, :]
        return carry_unused

    lax.fori_loop(0, n_groups, group, 0)

    ones = jnp.ones((n_heads, PAGE_SIZE), bf16)

    def finish(b, carry_unused):
        r = acc_ref[b]
        hi = r.astype(bf16)
        mid = (r - hi.astype(f32)).astype(bf16)
        lo = (r - hi.astype(f32) - mid.astype(f32)).astype(bf16)
        pv = _dot_nt(ones, hi) + _dot_nt(ones, mid) + _dot_nt(ones, lo)
        out = (pv + w_ref[b] * vn_ref[b]) / l_ref[b]
        o_ref[b] = jnp.sum(jnp.where(own, out, 0.0), axis=0, keepdims=True)
        return carry_unused

    lax.fori_loop(0, db, finish, 0)


def _fox_sample_pv(page_table, flags, e, w, l, v_new, cache_v, layer):
    db, n_pages = page_table.shape
    n_heads = e.shape[1]
    d_attn = v_new.shape[1]
    hbm = pl.BlockSpec(memory_space=pl.ANY)
    vn3 = v_new.reshape(db, 1, d_attn)
    whole2 = lambda shape: pl.BlockSpec(shape, lambda i, pt, fl: (0,) * len(shape))
    pv_pages = PV_PAGES_PER_STEP
    out = pl.pallas_call(
        functools.partial(_fox_sample_pv_kernel, layer=layer, n_heads=n_heads, pages=pv_pages),
        grid_spec=pltpu.PrefetchScalarGridSpec(
            num_scalar_prefetch=2,
            grid=(1,),
            in_specs=[whole2(e.shape), whole2(w.shape), whole2(l.shape), whole2(vn3.shape), hbm],
            out_specs=whole2((db, 1, d_attn)),
            scratch_shapes=[
                pltpu.VMEM((PV_SLOTS, pv_pages, d_attn, PAGE_SIZE), f32),
                pltpu.SemaphoreType.DMA((PV_SLOTS,)),
                pltpu.VMEM((db, d_attn, PAGE_SIZE), f32),
                pltpu.SMEM((db * n_pages,), jnp.int32),
                pltpu.SMEM((db * n_pages,), jnp.int32)],
        ),
        out_shape=jax.ShapeDtypeStruct((db, 1, d_attn), f32),
        compiler_params=_cparams(("arbitrary",)),
        name="fox_sample_pv",
    )(page_table, flags.reshape(db, n_pages), e, w, l, vn3, cache_v)
    return out.reshape(db, d_attn)


def _sgu_kernel(x_ref, g_ref, win_ref, lg_ref, lb_ref, ws_ref, bs_ref, wout_ref, *out_refs, tm, sample):
    d_sgu = wout_ref.shape[0]
    x = x_ref[...]
    h = _rms(x, g_ref[...]).astype(bf16)
    u = _gelu(_dot(h, win_ref[:, 0:d_sgu]))
    v = _gelu(_dot(h, win_ref[:, d_sgu:2 * d_sgu]))
    vn = _layernorm(v, lg_ref[...], lb_ref[...])
    if sample:
        o_ref, vn_ref = out_refs
        vn_ref[...] = vn
        gated = u * (vn * ws_ref[...] + bs_ref[...])
    else:
        (o_ref,) = out_refs
        vnb = vn.astype(bf16)
        group = d_sgu // N_SGU_GROUPS
        rows = []
        for r in range(tm // CHUNK):
            rs = slice(r * CHUNK, (r + 1) * CHUNK)
            cols = [_dot(ws_ref[g], vnb[rs, g * group:(g + 1) * group]) for g in range(N_SGU_GROUPS)]
            rows.append(jnp.concatenate(cols, axis=1) + bs_ref[...])
        mixed = jnp.concatenate(rows, axis=0)
        gated = u * mixed
    o_ref[...] = x + _dot(gated.astype(bf16), wout_ref[...])


def _sgu(x, g, win, lg, lb, ws, bs, wout, *, tm, sample):
    m, d = x.shape
    d_sgu = wout.shape[0]
    kern = functools.partial(_sgu_kernel, tm=tm, sample=sample)
    out_shape = [jax.ShapeDtypeStruct((m, d), f32)]
    out_specs = [_row_spec(tm, d)]
    if sample:
        out_shape.append(jax.ShapeDtypeStruct((m, d_sgu), f32))
        out_specs.append(_row_spec(tm, d_sgu))
    outs = pl.pallas_call(
        kern,
        grid=(m // tm,),
        in_specs=[_row_spec(tm, d), _const_spec(g.shape), _const_spec(win.shape), _const_spec(lg.shape),
                  _const_spec(lb.shape), _const_spec(ws.shape), _const_spec(bs.shape), _const_spec(wout.shape)],
        out_specs=tuple(out_specs),
        out_shape=tuple(out_shape),
        compiler_params=_cparams(("parallel",)),
        name="sgu_mixer_sample" if sample else "sgu_mixer",
    )(x, g, win, lg, lb, ws, bs, wout)
    return outs if sample else outs[0]


def _ffn_kernel(x_ref, g_ref, wg_ref, wu_ref, wd_ref, gf_ref, o_ref, *, tf, final):
    x = x_ref[...]
    h = _rms(x, g_ref[...]).astype(bf16)
    acc = x
    d_ff = wg_ref.shape[1]
    for c in range(d_ff // tf):
        sl = slice(c * tf, (c + 1) * tf)
        a = _silu(_dot(h, wg_ref[:, sl])) * _dot(h, wu_ref[:, sl])
        acc = acc + _dot(a.astype(bf16), wd_ref[sl, :])
    if final:
        acc = _rms(acc, gf_ref[...])
    o_ref[...] = acc


def _ffn(x, g, wg, wu, wd, gf, *, tm, final):
    m, d = x.shape
    d_ff = wg.shape[1]
    tf = d_ff // 2 if (d_ff // 2) % LANES == 0 else d_ff
    kern = functools.partial(_ffn_kernel, tf=tf, final=final)
    return pl.pallas_call(
        kern,
        grid=(m // tm,),
        in_specs=[_row_spec(tm, d), _const_spec(g.shape), _const_spec(wg.shape), _const_spec(wu.shape),
                  _const_spec(wd.shape), _const_spec(gf.shape)],
        out_specs=_row_spec(tm, d),
        out_shape=jax.ShapeDtypeStruct((m, d), f32),
        compiler_params=_cparams(("parallel",)),
        name="swiglu_ffn_final" if final else "swiglu_ffn",
    )(x, g, wg, wu, wd, gf)


def _ffn_scores_kernel(pt_ref, x_ref, g_ref, wg_ref, wu_ref, wd_ref, q_ref, kn_ref, lfn_ref, ck_ref, cf_ref,
                       o_ref, e_ref, w_ref, l_ref, fl_ref, kbuf, fbuf, sems, s_ref,
                       *, layer, n_heads, pages, chunk_bounds):
    i = pl.program_id(0)
    n_seq, n_pages = pt_ref.shape
    n_groups = n_pages // pages
    d_attn = q_ref.shape[-1]

    def page_copies(b, j, slot, p):
        page = pt_ref[b, n_pages - 1 - (j * pages + p)]
        return (pltpu.make_async_copy(ck_ref.at[layer, page], kbuf.at[slot, p], sems.at[0, slot]),
                pltpu.make_async_copy(cf_ref.at[layer, page], fbuf.at[slot, p], sems.at[1, slot]))

    def start_group(b, j, slot):
        def one(p, c):
            for cp in page_copies(b, j, slot, p):
                cp.start()
            return c
        lax.fori_loop(0, pages, one, 0)

    def wait_group(b, j, slot):
        def one(p, c):
            for cp in page_copies(b, j, slot, p):
                cp.wait()
            return c
        lax.fori_loop(0, pages, one, 0)

    @pl.when(i == 0)
    def _():
        start_group(0, 0, 0)

    x = x_ref[...]
    h = _rms(x, g_ref[...]).astype(bf16)
    acc = x
    q = q_ref[...].astype(f32)
    qbd32 = jnp.where(_own_head_mask(n_heads, d_attn), jnp.broadcast_to(q, (n_heads, d_attn)), 0.0)
    qbd = qbd32.astype(bf16)
    s_new = jnp.sum(qbd32 * kn_ref[...], axis=1, keepdims=True) - lfn_ref[...]
    carry = jnp.zeros((n_heads, 1), f32)
    pmax = jnp.full((n_heads, n_pages), NEG, f32)
    next_seq = jnp.minimum(i + 1, n_seq - 1)
    for j in range(n_groups):
        if j + 1 < n_groups:
            start_group(i, j + 1, (j + 1) % 2)
        else:
            start_group(next_seq, 0, 0)
        wait_group(i, j, j % 2)
        carry, pmax = _group_logits(qbd, kbuf, fbuf, j % 2, j, carry, pmax, s_ref, n_heads=n_heads, pages=pages)
        sl = slice(chunk_bounds[j], chunk_bounds[j + 1])
        a = _silu(_dot(h, wg_ref[:, sl])) * _dot(h, wu_ref[:, sl])
        acc = acc + _dot(a.astype(bf16), wd_ref[sl, :])
    o_ref[...] = acc
    e_ref[...], w_ref[...], l_ref[...], fl_ref[...] = _normalise_logits(s_new, pmax, s_ref)

    @pl.when(i == n_seq - 1)
    def _():
        wait_group(next_seq, 0, 0)


def _ffn_with_scores(x, g, wg, wu, wd, page_table, q, k_new, lf_new, cache_k, cache_f_t, layer, *, tm):
    m, d = x.shape
    d_ff = wg.shape[1]
    db, n_pages = page_table.shape
    n_heads = lf_new.shape[1]
    d_attn = q.shape[1]
    n_past = n_pages * PAGE_SIZE
    pages = PAGES_PER_STEP
    n_groups = n_pages // pages
    assert m // tm == db and n_pages % pages == 0 and n_groups % 2 == 0
    col_tile = 2 * LANES
    n_col = d_ff // col_tile
    assert d_ff % col_tile == 0 and n_col >= n_groups
    chunk_bounds = tuple(col_tile * ((n_col * c) // n_groups) for c in range(n_groups + 1))
    hbm = pl.BlockSpec(memory_space=pl.ANY)
    const = lambda a: pl.BlockSpec(a.shape, lambda i, pt: (0,) * a.ndim, pipeline_mode=pl.Buffered(1))
    per_seq = lambda *tail: pl.BlockSpec((None,) + tail, lambda i, pt: (i,) + (0,) * len(tail))
    kern = functools.partial(_ffn_scores_kernel, layer=layer, n_heads=n_heads, pages=pages,
                             chunk_bounds=chunk_bounds)
    return pl.pallas_call(
        kern,
        grid_spec=pltpu.PrefetchScalarGridSpec(
            num_scalar_prefetch=1,
            grid=(db,),
            in_specs=[pl.BlockSpec((tm, d), lambda i, pt: (i, 0)), const(g), const(wg), const(wu), const(wd),
                      per_seq(1, d_attn), per_seq(1, d_attn), per_seq(n_heads, 1), hbm, hbm],
            out_specs=(pl.BlockSpec((tm, d), lambda i, pt: (i, 0)), per_seq(n_heads, n_past), per_seq(n_heads, 1),
                       per_seq(n_heads, 1), per_seq(1, n_pages)),
            scratch_shapes=[
                pltpu.VMEM((2, pages, d_attn, PAGE_SIZE), f32),
                pltpu.VMEM((2, pages, n_heads, PAGE_SIZE), f32),
                pltpu.SemaphoreType.DMA((2, 2)),
                pltpu.VMEM((n_heads, n_past), f32)],
        ),
        out_shape=(jax.ShapeDtypeStruct((m, d), f32),
                   jax.ShapeDtypeStruct((db, n_heads, n_past), f32), jax.ShapeDtypeStruct((db, n_heads, 1), f32),
                   jax.ShapeDtypeStruct((db, n_heads, 1), f32), jax.ShapeDtypeStruct((db, 1, n_pages), jnp.int32)),
        compiler_params=_cparams(("arbitrary",)),
        name="swiglu_ffn_with_sample_scores",
    )(page_table, x, g, wg, wu, wd, q.reshape(db, 1, d_attn), k_new.reshape(db, 1, d_attn),
      lf_new.reshape(db, n_heads, 1), cache_k, cache_f_t)


def kernel(x_prompt, x_sample, cache_k, cache_v, cache_logf, page_table, state_conv, norm_mix, norm_ffn, norm_final, w_in_even, b_forget, conv_w, conv_b, conv_ln_g, conv_ln_b, w_out_even, w_in_odd, sgu_ln_g, sgu_ln_b, sgu_w, sgu_b, w_out_odd, w_gate, w_up, w_down):
    batch, seq, d_model = x_prompt.shape
    db, dec_seq, _ = x_sample.shape
    assert dec_seq == 1, "the sample path handles one new token per sequence"
    depth = norm_mix.shape[0]
    n_heads = b_forget.shape[1]
    d_attn = n_heads * HEAD_DIM
    d_conv = conv_w.shape[2]
    n_phys = cache_k.shape[1]
    n_tok = batch * seq
    row = lambda a: a.reshape(1, -1)

    xp = x_prompt.reshape(n_tok, d_model)
    xs = x_sample.reshape(db, d_model)
    tm_p = ROW_TILE
    tm_s = db

    feature_major = lambda a: jnp.transpose(a, (0, 1, 3, 4, 2)).reshape(a.shape[0], n_phys, d_attn, PAGE_SIZE)
    cache_kt = feature_major(cache_k)
    cache_vt = feature_major(cache_v)
    cache_f_t = jnp.swapaxes(cache_logf.astype(f32), 2, 3)

    kp_l, vp_l, fp_l, cp_l = [], [], [], []
    ks_l, vs_l, fs_l, cs_l = [], [], [], []
    sgu_l = []
    for l in range(depth):
        g_mix = row(norm_mix[l])
        last = l == depth - 1
        if l % 2 == 0:
            i = l // 2
            w_in = w_in_even[i]
            w_in_t = w_in.T
            wq = w_in[:, 0:d_attn].astype(bf16)
            wkt = w_in_t[d_attn:2 * d_attn].astype(bf16)
            wvt = w_in_t[2 * d_attn:3 * d_attn].astype(bf16)
            o = 3 * d_attn
            wft = w_in_t[o:o + n_heads].astype(bf16)
            wglu = w_in[:, o + n_heads:].astype(bf16)
            wa = w_out_even[i][0:d_attn].astype(bf16)
            wc = w_out_even[i][d_attn:].astype(bf16)
            cw, cb, lg, lb = conv_w[i], row(conv_b[i]), row(conv_ln_g[i]), row(conv_ln_b[i])

            qb, ktb, vtb, kt, vt, lf_t, glu = _even_in_prompt(
                xp, g_mix, wq, wkt, wvt, wft, wglu, b_forget[i].reshape(n_heads, 1), tm=tm_p, batch=batch, seq=seq)
            c = _cumsum_lanes(lf_t)
            attn = _fox_prompt(qb, ktb, vtb, c, batch=batch, seq=seq)
            xp = _even_out(xp, attn, glu, cw, cb, lg, lb, wa, wc, tm=tm_p, seq=seq)
            token_major = lambda a: jnp.transpose(a.reshape(batch, n_heads, HEAD_DIM, seq), (0, 3, 1, 2))
            kp_l.append(token_major(kt))
            vp_l.append(token_major(vt))
            fp_l.append(jnp.swapaxes(lf_t, 1, 2))
            cp_l.append(glu.reshape(batch, seq, d_conv)[:, seq - (CONV_WIDTH - 1):])

            qb2, k2, v2, lf2, glu2 = _even_in_sample(xs, g_mix, wq, wkt, wvt, wft, wglu, row(b_forget[i]))
            state_t = jnp.swapaxes(state_conv[i], 0, 1)

            def finish_sample_mixer(xs, scores, v2=v2, glu2=glu2, state_t=state_t, i=i,
                                    cw=cw, cb=cb, lg=lg, lb=lb, wa=wa, wc=wc):
                e, w, l_sum, flags = scores
                attn2 = _fox_sample_pv(page_table, flags.reshape(db, -1), e, w, l_sum, v2, cache_vt, i)
                return _even_out_sample(xs, attn2, glu2, state_t, cw, cb, lg, lb, wa, wc)

            assert not last, "an even layer is followed by an odd one in this trunk"
            ride_along = True
            ks_l.append(k2.reshape(db, 1, n_heads, HEAD_DIM))
            vs_l.append(v2.reshape(db, 1, n_heads, HEAD_DIM))
            fs_l.append(lf2.reshape(db, 1, n_heads))
            cs_l.append(jnp.concatenate([state_conv[i][:, 1:], glu2[:, None, :]], axis=1))
        else:
            ride_along = False
            j = l // 2
            d_sgu = w_out_odd.shape[1]
            group = d_sgu // N_SGU_GROUPS
            win = w_in_odd[j].astype(bf16)
            wout = w_out_odd[j].astype(bf16)
            lg, lb = row(sgu_ln_g[j]), row(sgu_ln_b[j])
            causal = jnp.tril(jnp.ones((CHUNK, CHUNK), dtype=bool))
            ws = jnp.where(causal, sgu_w[j], 0).astype(bf16)
            bs = jnp.repeat(jnp.swapaxes(sgu_b[j], 0, 1), group, axis=1)
            xp = _sgu(xp, g_mix, win, lg, lb, ws, bs, wout, tm=tm_p, sample=False)
            ws0 = row(jnp.repeat(sgu_w[j][:, 0, 0], group))
            bs0 = row(jnp.repeat(sgu_b[j][:, 0], group))
            xs, vn_s = _sgu(xs, g_mix, win, lg, lb, ws0, bs0, wout, tm=tm_s, sample=True)
            sgu_l.append(vn_s.reshape(db, 1, d_sgu))
        g_ffn = row(norm_ffn[l])
        gf = row(norm_final)
        wg, wu, wd = w_gate[l].astype(bf16), w_up[l].astype(bf16), w_down[l].astype(bf16)
        if ride_along:
            xp, *scores = _ffn_with_scores(xp, g_ffn, wg, wu, wd, page_table, qb2, k2, lf2, cache_kt, cache_f_t,
                                           l // 2, tm=tm_p)
            xs = finish_sample_mixer(xs, scores)
        else:
            xp = _ffn(xp, g_ffn, wg, wu, wd, gf, tm=FFN_ROW_TILE, final=last)
        xs = _ffn(xs, g_ffn, wg, wu, wd, gf, tm=tm_s, final=last)

    return (xp.reshape(batch, seq, d_model), xs.reshape(db, 1, d_model),
            jnp.stack(kp_l), jnp.stack(vp_l), jnp.stack(fp_l), jnp.stack(cp_l),
            jnp.stack(ks_l), jnp.stack(vs_l), jnp.stack(fs_l), jnp.stack(cs_l),
            jnp.stack(sgu_l))
```
